```python
import math
import jax, jax.numpy as jnp
from jax import lax
import numpy as np

D_MODEL = 2048
BATCH = 2
SEQ = 16384
DEPTH = 2

D_MIX = D_MODEL
A_HEAD_DIM = 64
A_HEADS = (3 * D_MIX // 8) // A_HEAD_DIM
A_WIDTH = A_HEADS * A_HEAD_DIM
A_PATTERNS = ((128, 1), (512, 4), (2048, 16))
A_BLOCK = 128
B_WINDOWS = (2, 4, 8, 16)
B_GROUPS = len(B_WINDOWS)
B_WIDTH = D_MIX // 4
B_GROUP_DIM = B_WIDTH // B_GROUPS
C_HEAD_DIM = 64
C_V_DIM = 2 * C_HEAD_DIM
C_WIDTH = D_MIX - A_WIDTH - B_WIDTH
C_HEADS = C_WIDTH // C_V_DIM
C_QK_WIDTH = C_HEADS * 2 * C_HEAD_DIM
C_QBLOCK = 128
IN_COLS = 3 * A_WIDTH + B_WIDTH + 2 * C_QK_WIDTH + C_WIDTH
D_FF = 5632
EPS = 1e-6
NEG_INF = -1e30

kernel_name = "hymba_style_dilated_pool_diffattn_macaron"


def rmsnorm(x, g):
    xf = x.astype(jnp.float32)
    y = xf * lax.rsqrt(jnp.mean(xf * xf, axis=-1, keepdims=True) + EPS)
    return (y * g.astype(jnp.float32)).astype(x.dtype)


def swiglu(h, w_gate, w_up, w_down):
    return (jax.nn.silu(h @ w_gate) * (h @ w_up)) @ w_down


def dilated_window_attention(q, k, v, window, dilation):
    bsz, seq, heads, dh = q.shape
    steps = window // dilation
    sub = seq // dilation
    nb = -(-sub // A_BLOCK)
    pad = nb * A_BLOCK - sub

    def regroup(a):
        a = a.reshape(bsz, sub, dilation, heads, dh).transpose(0, 2, 3, 1, 4)
        return jnp.pad(a, ((0, 0), (0, 0), (0, 0), (0, pad), (0, 0)))

    def window_blocks(a):
        a = jnp.pad(a, ((0, 0), (0, 0), (0, 0), (A_BLOCK, 0), (0, 0)))
        a = a.reshape(bsz, dilation, heads, nb + 1, A_BLOCK, dh)
        return jnp.concatenate([a[:, :, :, :-1], a[:, :, :, 1:]], axis=4)

    qb = regroup(q).reshape(bsz, dilation, heads, nb, A_BLOCK, dh)
    kb = window_blocks(regroup(k))
    vb = window_blocks(regroup(v))
    s = jnp.einsum('brhnqe,brhnke->brhnqk', qb, kb).astype(jnp.float32) * (dh ** -0.5)
    qi = jnp.arange(A_BLOCK)[:, None] + A_BLOCK
    ki = jnp.arange(2 * A_BLOCK)[None, :]
    dist = qi - ki
    blk = jnp.arange(nb)[:, None, None]
    valid = (dist >= 0) & (dist <= steps) & (blk * A_BLOCK + ki - A_BLOCK >= 0)
    s = jnp.where(valid, s, NEG_INF)
    lse = jax.nn.logsumexp(s, axis=-1)
    p = jnp.exp(s - lse[..., None])
    o = jnp.einsum('brhnqk,brhnke->brhnqe', p.astype(v.dtype), vb)
    o = o.reshape(bsz, dilation, heads, nb * A_BLOCK, dh)[:, :, :, :sub]
    o = o.transpose(0, 3, 1, 2, 4).reshape(bsz, seq, heads, dh)
    lse = lse.reshape(bsz, dilation, heads, nb * A_BLOCK)[:, :, :, :sub]
    lse = lse.transpose(0, 3, 1, 2).reshape(bsz, seq, heads)
    return o, lse


def dilated_mixture_attention(q, k, v):
    outs, lses = [], []
    for window, dilation in A_PATTERNS:
        o, l = dilated_window_attention(q, k, v, window, dilation)
        outs.append(o)
        lses.append(l)
    wts = jax.nn.softmax(jnp.stack(lses, axis=0), axis=0)
    o = jnp.einsum('pbsh,pbshe->bshe', wts, jnp.stack(outs, axis=0).astype(jnp.float32))
    return o.astype(q.dtype)


def multiscale_pool(u, w_groups, scale):
    bsz, seq, _ = u.shape
    ug = u.reshape(bsz, seq, B_GROUPS, B_GROUP_DIM).astype(jnp.float32)
    c = jnp.pad(jnp.cumsum(ug, axis=1), ((0, 0), (1, 0), (0, 0), (0, 0)))
    t = jnp.arange(seq)
    pooled = []
    for g, w in enumerate(B_WINDOWS):
        upper = c[:, 1:, g]
        lower = c[:, jnp.maximum(t + 1 - w, 0), g]
        cnt = jnp.minimum(t + 1, w).astype(jnp.float32)
        pooled.append((upper - lower) / cnt[None, :, None])
    pooled = jnp.stack(pooled, axis=2)
    diff = (pooled - ug).astype(u.dtype)
    y = jnp.einsum('bsgc,gcd->bsgd', diff, w_groups).reshape(bsz, seq, B_WIDTH)
    return y * scale


def diff_attention(q, k, v, lam, lambda_init, subln_g):
    bsz, seq, heads, _, dh = q.shape
    nqb = seq // C_QBLOCK
    qb = q.reshape(bsz, nqb, C_QBLOCK, heads, 2, dh).transpose(1, 0, 2, 3, 4, 5)
    k_pos = jnp.arange(seq)
    scale = dh ** -0.5

    def block(args):
        qblk, i = args
        s = jnp.einsum('bqhme,bkhme->bhmqk', qblk, k).astype(jnp.float32) * scale
        q_pos = i * C_QBLOCK + jnp.arange(C_QBLOCK)
        causal = k_pos[None, :] <= q_pos[:, None]
        p = jax.nn.softmax(jnp.where(causal, s, NEG_INF), axis=-1)
        a = p[:, :, 0] - lam * p[:, :, 1]
        return jnp.einsum('bhqk,bkhd->bqhd', a.astype(v.dtype), v)

    o = lax.map(block, (qb, jnp.arange(nqb)))
    o = o.transpose(1, 0, 2, 3, 4).reshape(bsz, seq, heads, C_V_DIM)
    return rmsnorm(o, subln_g) * (1.0 - lambda_init)


def setup_inputs(seed: int = 0) -> dict:
    key = jax.random.key(seed)
    ks = jax.random.split(key, 24)
    f32 = jnp.float32

    def nrm(k, shape, scale):
        return jax.random.normal(k, shape, f32) * scale

    def gain(k, shape):
        return 1.0 + 0.02 * jax.random.normal(k, shape, f32)

    return {
        "x": jax.random.normal(ks[0], (BATCH, SEQ, D_MODEL), f32),
        "ffn1_norm": gain(ks[1], (DEPTH, D_MODEL)),
        "ffn1_w_gate": nrm(ks[2], (DEPTH, D_MODEL, D_FF), D_MODEL ** -0.5),
        "ffn1_w_up": nrm(ks[3], (DEPTH, D_MODEL, D_FF), D_MODEL ** -0.5),
        "ffn1_w_down": nrm(ks[4], (DEPTH, D_FF, D_MODEL), D_FF ** -0.5),
        "mix_norm": gain(ks[5], (DEPTH, D_MODEL)),
        "w_in": nrm(ks[6], (DEPTH, D_MODEL, IN_COLS), D_MODEL ** -0.5),
        "pool_w": nrm(ks[7], (DEPTH, B_GROUPS, B_GROUP_DIM, B_GROUP_DIM), B_GROUP_DIM ** -0.5),
        "pool_scale": gain(ks[8], (DEPTH, B_WIDTH)),
        "diff_lambda_q1": nrm(ks[9], (DEPTH, C_HEAD_DIM), 0.1),
        "diff_lambda_k1": nrm(ks[10], (DEPTH, C_HEAD_DIM), 0.1),
        "diff_lambda_q2": nrm(ks[11], (DEPTH, C_HEAD_DIM), 0.1),
        "diff_lambda_k2": nrm(ks[12], (DEPTH, C_HEAD_DIM), 0.1),
        "diff_subln": gain(ks[13], (DEPTH, C_V_DIM)),
        "w_out": nrm(ks[14], (DEPTH, D_MIX, D_MODEL), D_MIX ** -0.5),
        "ffn2_norm": gain(ks[15], (DEPTH, D_MODEL)),
        "ffn2_w_gate": nrm(ks[16], (DEPTH, D_MODEL, D_FF), D_MODEL ** -0.5),
        "ffn2_w_up": nrm(ks[17], (DEPTH, D_MODEL, D_FF), D_MODEL ** -0.5),
        "ffn2_w_down": nrm(ks[18], (DEPTH, D_FF, D_MODEL), D_FF ** -0.5),
        "final_norm": gain(ks[19], (D_MODEL,)),
    }


def reference(x, ffn1_norm, ffn1_w_gate, ffn1_w_up, ffn1_w_down, mix_norm, w_in,
              pool_w, pool_scale, diff_lambda_q1, diff_lambda_k1, diff_lambda_q2,
              diff_lambda_k2, diff_subln, w_out, ffn2_norm, ffn2_w_gate, ffn2_w_up,
              ffn2_w_down, final_norm):
    bsz, seq, _ = x.shape
    splits = [3 * A_WIDTH, 3 * A_WIDTH + B_WIDTH, 3 * A_WIDTH + B_WIDTH + C_QK_WIDTH,
              3 * A_WIDTH + B_WIDTH + 2 * C_QK_WIDTH]
    for l in range(DEPTH):
        x = x + 0.5 * swiglu(rmsnorm(x, ffn1_norm[l]), ffn1_w_gate[l], ffn1_w_up[l], ffn1_w_down[l])

        h = rmsnorm(x, mix_norm[l])
        proj = h @ w_in[l]
        a_qkv, b_u, c_q, c_k, c_v = jnp.split(proj, splits, axis=-1)

        a_qkv = a_qkv.reshape(bsz, seq, 3, A_HEADS, A_HEAD_DIM)
        a_out = dilated_mixture_attention(a_qkv[:, :, 0], a_qkv[:, :, 1], a_qkv[:, :, 2])

        b_out = multiscale_pool(b_u, pool_w[l], pool_scale[l])

        lambda_init = 0.8 - 0.6 * math.exp(-0.3 * l)
        lam = (jnp.exp(jnp.sum(diff_lambda_q1[l].astype(jnp.float32) * diff_lambda_k1[l].astype(jnp.float32)))
               - jnp.exp(jnp.sum(diff_lambda_q2[l].astype(jnp.float32) * diff_lambda_k2[l].astype(jnp.float32)))
               + lambda_init)
        c_out = diff_attention(c_q.reshape(bsz, seq, C_HEADS, 2, C_HEAD_DIM),
                               c_k.reshape(bsz, seq, C_HEADS, 2, C_HEAD_DIM),
                               c_v.reshape(bsz, seq, C_HEADS, C_V_DIM),
                               lam, lambda_init, diff_subln[l])

        mixed = jnp.concatenate([a_out.reshape(bsz, seq, A_WIDTH).astype(x.dtype),
                                 b_out.astype(x.dtype),
                                 c_out.reshape(bsz, seq, C_WIDTH).astype(x.dtype)], axis=-1)
        x = x + mixed @ w_out[l]

        x = x + 0.5 * swiglu(rmsnorm(x, ffn2_norm[l]), ffn2_w_gate[l], ffn2_w_up[l], ffn2_w_down[l])
    return rmsnorm(x, final_norm)
```

```python
import functools
import math

import jax
import jax.numpy as jnp
from jax import lax
from jax.experimental import pallas as pl
from jax.experimental.pallas import tpu as pltpu

F32 = jnp.float32
BF16 = jnp.bfloat16

EPS = 1e-6
NEG_INF = -1e30

HEAD_DIM = 64
LANES = 128
A_HEADS = 12
A_WIDTH = A_HEADS * HEAD_DIM
A_PATTERNS = ((128, 1), (512, 4), (2048, 16))
A_BLOCK = 128
B_WINDOWS = (2, 4, 8, 16)
B_GROUP_DIM = 128
B_WIDTH = len(B_WINDOWS) * B_GROUP_DIM
B_HALO = 16
C_HEADS = 6
C_V_DIM = 2 * HEAD_DIM
C_WIDTH = C_HEADS * C_V_DIM
MAIN_COLS = 3 * A_WIDTH + 3 * C_WIDTH
QK_SCALE = HEAD_DIM ** -0.5

VMEM_LIMIT_BYTES = 56 * 1024 * 1024


def _params(n_axes):
    return pltpu.CompilerParams(
        dimension_semantics=("arbitrary",) * n_axes,
        vmem_limit_bytes=VMEM_LIMIT_BYTES,
    )


def _rmsnorm_rows(x, g):
    ms = jnp.mean(x * x, axis=-1, keepdims=True)
    return x * lax.rsqrt(ms + EPS) * g


def _ffn_kernel(x_ref, g_ref, wg_ref, wu_ref, wd_ref, fg_ref, o_ref, h_ref, *, final_norm):
    f = pl.program_id(1)

    @pl.when(f == 0)
    def _():
        x = x_ref[...]
        h_ref[...] = _rmsnorm_rows(x, g_ref[...]).astype(BF16)
        o_ref[...] = x

    h = h_ref[...]
    gate = jnp.dot(h, wg_ref[...], preferred_element_type=F32)
    up = jnp.dot(h, wu_ref[...], preferred_element_type=F32)
    act = (gate * (1.0 / (1.0 + jnp.exp(-gate))) * up).astype(BF16)
    o_ref[...] += 0.5 * jnp.dot(act, wd_ref[...], preferred_element_type=F32)

    if final_norm:
        @pl.when(f == pl.num_programs(1) - 1)
        def _():
            o_ref[...] = _rmsnorm_rows(o_ref[...], fg_ref[...])


def _ffn(x, g, wg, wu, wd, final_g, *, final_norm, tm=512, tf=512):
    t, d = x.shape
    d_ff = wg.shape[1]
    assert t % tm == 0 and d_ff % tf == 0
    return pl.pallas_call(
        functools.partial(_ffn_kernel, final_norm=final_norm),
        grid=(t // tm, d_ff // tf),
        in_specs=[
            pl.BlockSpec((tm, d), lambda i, f: (i, 0)),
            pl.BlockSpec((1, d), lambda i, f: (0, 0)),
            pl.BlockSpec((d, tf), lambda i, f: (0, f)),
            pl.BlockSpec((d, tf), lambda i, f: (0, f)),
            pl.BlockSpec((tf, d), lambda i, f: (f, 0)),
            pl.BlockSpec((1, d), lambda i, f: (0, 0)),
        ],
        out_specs=pl.BlockSpec((tm, d), lambda i, f: (i, 0)),
        out_shape=jax.ShapeDtypeStruct((t, d), F32),
        scratch_shapes=[pltpu.VMEM((tm, d), BF16)],
        compiler_params=_params(2),
        name="ffn_final" if final_norm else "ffn",
    )(x, g, wg, wu, wd, final_g)


def _norm_matmul_kernel(x_ref, g_ref, w_ref, o_ref, h_ref):
    @pl.when(pl.program_id(1) == 0)
    def _():
        h_ref[...] = _rmsnorm_rows(x_ref[...], g_ref[...]).astype(BF16)

    o_ref[...] = jnp.dot(h_ref[...], w_ref[...], preferred_element_type=F32).astype(o_ref.dtype)


def _norm_matmul(x, g, w, out_dtype, *, tm=1024, tn=512, name):
    t, d = x.shape
    n = w.shape[1]
    tm = min(tm, t)
    assert t % tm == 0 and n % tn == 0
    return pl.pallas_call(
        _norm_matmul_kernel,
        grid=(t // tm, n // tn),
        in_specs=[
            pl.BlockSpec((tm, d), lambda i, j: (i, 0)),
            pl.BlockSpec((1, d), lambda i, j: (0, 0)),
            pl.BlockSpec((d, tn), lambda i, j: (0, j)),
        ],
        out_specs=pl.BlockSpec((tm, tn), lambda i, j: (i, j)),
        out_shape=jax.ShapeDtypeStruct((t, n), out_dtype),
        scratch_shapes=[pltpu.VMEM((tm, d), BF16)],
        compiler_params=_params(2),
        name=name,
    )(x, g, w)


def _dilated_kernel(q_ref, kp_ref, kc_ref, vp_ref, vc_ref, o_ref, lse_ref):
    blk = pl.program_id(1)
    qi = lax.broadcasted_iota(jnp.int32, (A_BLOCK, 2 * A_BLOCK), 0)
    kj = lax.broadcasted_iota(jnp.int32, (A_BLOCK, 2 * A_BLOCK), 1)
    lowest = jnp.where(blk > 0, qi, A_BLOCK)
    valid = (kj >= lowest) & (kj <= qi + A_BLOCK)
    lane = lax.broadcasted_iota(jnp.int32, (A_BLOCK, LANES), 1)
    low_half = lane < HEAD_DIM

    q_all = q_ref[0] * QK_SCALE
    for hp in range(A_HEADS // 2):
        sl = slice(hp * LANES, (hp + 1) * LANES)
        q2 = q_all[:, sl]
        k2 = jnp.concatenate([kp_ref[0, :, sl], kc_ref[0, :, sl]], axis=0)
        v2 = jnp.concatenate([vp_ref[0, :, sl], vc_ref[0, :, sl]], axis=0)
        outs, lses = [], []
        for half in range(2):
            qm = jnp.where(low_half, q2, 0) if half == 0 else jnp.where(low_half, 0, q2)
            s = lax.dot_general(qm, k2, (((1,), (1,)), ((), ())), preferred_element_type=F32)
            s = jnp.where(valid, s, NEG_INF)
            m = jnp.max(s, axis=-1, keepdims=True)
            p = jnp.exp(s - m)
            l = jnp.sum(p, axis=-1, keepdims=True)
            o = jnp.dot(p.astype(BF16), v2, preferred_element_type=F32)
            outs.append(o / l)
            lses.append(jnp.broadcast_to(m + jnp.log(l), (A_BLOCK, LANES)))
        o_ref[0, :, sl] = jnp.where(low_half, outs[0], outs[1])
        lse_ref[0, :, sl] = jnp.where(low_half, lses[0], lses[1])


def _dilated_attention(proj, bsz, seq, dilation):
    sub = seq // dilation
    assert sub % A_BLOCK == 0
    nblk = sub // A_BLOCK
    cols = proj.shape[-1]
    cpb = cols // A_WIDTH
    view = proj.reshape(bsz, sub, dilation * cols)

    def spec(which, prev):
        if prev:
            return pl.BlockSpec((1, A_BLOCK, A_WIDTH),
                                lambda b, i, r: (b, jnp.maximum(i - 1, 0), r * cpb + which))
        return pl.BlockSpec((1, A_BLOCK, A_WIDTH), lambda b, i, r: (b, i, r * cpb + which))

    out_spec = pl.BlockSpec((1, A_BLOCK, A_WIDTH), lambda b, i, r: (b, i, r))
    out_shape = jax.ShapeDtypeStruct((bsz, sub, dilation * A_WIDTH), F32)
    o, lse = pl.pallas_call(
        _dilated_kernel,
        grid=(bsz, nblk, dilation),
        in_specs=[spec(0, False), spec(1, True), spec(1, False), spec(2, True), spec(2, False)],
        out_specs=[out_spec, out_spec],
        out_shape=[out_shape, out_shape],
        compiler_params=_params(3),
        name=f"dilated_attn_d{dilation}",
    )(view, view, view, view, view)
    return o.reshape(bsz * seq, A_WIDTH), lse.reshape(bsz * seq, A_WIDTH)


def _diff_attn_kernel(q_ref, k_ref, v_ref, lq1_ref, lk1_ref, lq2_ref, lk2_ref, g_ref, o_ref,
                      *, tq, lambda_init):
    qb = pl.program_id(2)
    lane = lax.broadcasted_iota(jnp.int32, (tq, LANES), 1)
    low_half = lane < HEAD_DIM
    q = q_ref[0] * QK_SCALE
    zeros = jnp.zeros_like(q)
    qs = (jnp.where(low_half, q, zeros), jnp.where(low_half, zeros, q))

    def chunk(j, carry, masked):
        start = pl.multiple_of(j * tq, tq)
        k = k_ref[0, pl.ds(start, tq), :]
        v = v_ref[0, pl.ds(start, tq), :]
        new = []
        for mi in range(2):
            m_old, l_old, acc_old = carry[mi]
            s = lax.dot_general(qs[mi], k, (((1,), (1,)), ((), ())), preferred_element_type=F32)
            if masked:
                row = lax.broadcasted_iota(jnp.int32, (tq, tq), 0)
                col = lax.broadcasted_iota(jnp.int32, (tq, tq), 1)
                s = jnp.where(col <= row, s, NEG_INF)
            m_new = jnp.maximum(m_old, jnp.max(s, axis=-1, keepdims=True))
            alpha = jnp.exp(m_old - m_new)
            p = jnp.exp(s - m_new)
            l_new = alpha * l_old + jnp.sum(p, axis=-1, keepdims=True)
            acc_new = alpha * acc_old + jnp.dot(p.astype(BF16), v, preferred_element_type=F32)
            new.append((m_new, l_new, acc_new))
        return tuple(new)

    init = tuple((jnp.full((tq, 1), NEG_INF, F32), jnp.zeros((tq, 1), F32),
                  jnp.zeros((tq, LANES), F32)) for _ in range(2))
    carry = lax.fori_loop(0, qb, lambda j, c: chunk(j, c, False), init)
    (_, l1, acc1), (_, l2, acc2) = chunk(qb, carry, True)

    lam = (jnp.exp(jnp.sum(lq1_ref[...] * lk1_ref[...], axis=-1, keepdims=True))
           - jnp.exp(jnp.sum(lq2_ref[...] * lk2_ref[...], axis=-1, keepdims=True))
           + lambda_init)
    o = acc1 / l1 - lam * (acc2 / l2)
    o_ref[0] = (_rmsnorm_rows(o, g_ref[...]) * (1.0 - lambda_init)).astype(o_ref.dtype)


def _diff_attention(proj, bsz, seq, lq1, lk1, lq2, lk2, subln_g, lambda_init, *, tq=512):
    tq = min(tq, seq)
    assert seq % tq == 0
    cols = proj.shape[-1]
    view = proj.reshape(bsz, seq, cols)
    q0 = 3 * A_WIDTH // LANES
    k0 = q0 + C_HEADS
    v0 = k0 + C_HEADS
    vec = lambda: pl.BlockSpec((1, HEAD_DIM), lambda b, h, i: (0, 0))
    out = pl.pallas_call(
        functools.partial(_diff_attn_kernel, tq=tq, lambda_init=lambda_init),
        grid=(bsz, C_HEADS, seq // tq),
        in_specs=[
            pl.BlockSpec((1, tq, LANES), lambda b, h, i: (b, i, q0 + h)),
            pl.BlockSpec((1, seq, LANES), lambda b, h, i: (b, 0, k0 + h)),
            pl.BlockSpec((1, seq, LANES), lambda b, h, i: (b, 0, v0 + h)),
            vec(), vec(), vec(), vec(),
            pl.BlockSpec((1, C_V_DIM), lambda b, h, i: (0, 0)),
        ],
        out_specs=pl.BlockSpec((1, tq, LANES), lambda b, h, i: (b, i, h)),
        out_shape=jax.ShapeDtypeStruct((bsz, seq, C_WIDTH), BF16),
        compiler_params=_params(3),
        name="diff_attn",
    )(view, view, view, lq1, lk1, lq2, lk2, subln_g)
    return out.reshape(bsz * seq, C_WIDTH)


def _mix_out_kernel(x_ref, o1_ref, o2_ref, o3_ref, l1_ref, l2_ref, l3_ref, u_ref, halo_ref,
                    c_ref, pw_ref, ps_ref, wo_ref, out_ref, *, tm, rows_per_seq):
    i = pl.program_id(0)

    l1, l2, l3 = l1_ref[...], l2_ref[...], l3_ref[...]
    mx = jnp.maximum(jnp.maximum(l1, l2), l3)
    e1, e2, e3 = jnp.exp(l1 - mx), jnp.exp(l2 - mx), jnp.exp(l3 - mx)
    den = e1 + e2 + e3
    a = (e1 / den) * o1_ref[...] + (e2 / den) * o2_ref[...] + (e3 / den) * o3_ref[...]
    acc = jnp.dot(a.astype(BF16), wo_ref[0:A_WIDTH, :], preferred_element_type=F32)

    t_in_seq = (i * tm) % rows_per_seq + lax.broadcasted_iota(jnp.int32, (tm, 1), 0)
    first = (i * tm) % rows_per_seq == 0
    halo = jnp.where(first, jnp.zeros((B_HALO, B_WIDTH), F32), halo_ref[...])
    u = u_ref[...]
    ext = jnp.concatenate([halo, u], axis=0)
    ys = []
    win = ext
    width = 1
    for g, w in enumerate(B_WINDOWS):
        while width < w:
            win = win + pltpu.roll(win, width, 0)
            width *= 2
        gs = slice(g * B_GROUP_DIM, (g + 1) * B_GROUP_DIM)
        cnt = jnp.minimum(t_in_seq + 1, w).astype(F32)
        diff = win[B_HALO:, gs] / cnt - u[:, gs]
        ys.append(jnp.dot(diff.astype(BF16), pw_ref[g], preferred_element_type=F32))
    b = jnp.concatenate(ys, axis=-1) * ps_ref[...]
    acc += jnp.dot(b.astype(BF16), wo_ref[A_WIDTH:A_WIDTH + B_WIDTH, :], preferred_element_type=F32)

    acc += jnp.dot(c_ref[...], wo_ref[A_WIDTH + B_WIDTH:, :], preferred_element_type=F32)
    out_ref[...] = x_ref[...] + acc


def _mix_out(x, a_outs, a_lses, u, c, pool_w, pool_scale, w_out, seq, *, tm=256):
    t, d = x.shape
    assert t % tm == 0 and seq % tm == 0 and tm % B_HALO == 0
    row = lambda width: pl.BlockSpec((tm, width), lambda i: (i, 0))
    halo_blocks = tm // B_HALO
    return pl.pallas_call(
        functools.partial(_mix_out_kernel, tm=tm, rows_per_seq=seq),
        grid=(t // tm,),
        in_specs=[
            row(d),
            row(A_WIDTH), row(A_WIDTH), row(A_WIDTH),
            row(A_WIDTH), row(A_WIDTH), row(A_WIDTH),
            row(B_WIDTH),
            pl.BlockSpec((B_HALO, B_WIDTH), lambda i: (jnp.maximum(i * halo_blocks - 1, 0), 0)),
            row(C_WIDTH),
            pl.BlockSpec(pool_w.shape, lambda i: (0, 0, 0)),
            pl.BlockSpec((1, B_WIDTH), lambda i: (0, 0)),
            pl.BlockSpec(w_out.shape, lambda i: (0, 0)),
        ],
        out_specs=row(d),
        out_shape=jax.ShapeDtypeStruct((t, d), F32),
        compiler_params=_params(1),
        name="mix_out",
    )(x, *a_outs, *a_lses, u, u, c, pool_w, pool_scale, w_out)


def kernel(x, ffn1_norm, ffn1_w_gate, ffn1_w_up, ffn1_w_down, mix_norm, w_in, pool_w, pool_scale,
           diff_lambda_q1, diff_lambda_k1, diff_lambda_q2, diff_lambda_k2, diff_subln, w_out,
           ffn2_norm, ffn2_w_gate, ffn2_w_up, ffn2_w_down, final_norm):
    bsz, seq, d = x.shape
    depth = w_in.shape[0]
    assert w_in.shape[2] == MAIN_COLS + B_WIDTH and d == A_WIDTH + B_WIDTH + C_WIDTH
    xt = x.reshape(bsz * seq, d)
    row = lambda v: v.reshape(1, -1).astype(F32)
    b_lo = 3 * A_WIDTH
    for l in range(depth):
        xt = _ffn(xt, row(ffn1_norm[l]), ffn1_w_gate[l].astype(BF16), ffn1_w_up[l].astype(BF16),
                  ffn1_w_down[l].astype(BF16), row(final_norm), final_norm=False)

        w_l = w_in[l]
        w_main = jnp.concatenate([w_l[:, :b_lo], w_l[:, b_lo + B_WIDTH:]], axis=1).astype(BF16)
        w_u = w_l[:, b_lo:b_lo + B_WIDTH].astype(BF16)
        g_mix = row(mix_norm[l])
        proj = _norm_matmul(xt, g_mix, w_main, BF16, name="proj_main")
        u = _norm_matmul(xt, g_mix, w_u, F32, name="proj_pool")

        a_outs, a_lses = [], []
        for window, dilation in A_PATTERNS:
            assert window // dilation == A_BLOCK
            o, lse = _dilated_attention(proj, bsz, seq, dilation)
            a_outs.append(o)
            a_lses.append(lse)

        lambda_init = 0.8 - 0.6 * math.exp(-0.3 * l)
        c = _diff_attention(proj, bsz, seq, row(diff_lambda_q1[l]), row(diff_lambda_k1[l]),
                            row(diff_lambda_q2[l]), row(diff_lambda_k2[l]), row(diff_subln[l]),
                            lambda_init)

        xt = _mix_out(xt, a_outs, a_lses, u, c, pool_w[l].astype(BF16), row(pool_scale[l]),
                      w_out[l].astype(BF16), seq)

        xt = _ffn(xt, row(ffn2_norm[l]), ffn2_w_gate[l].astype(BF16), ffn2_w_up[l].astype(BF16),
                  ffn2_w_down[l].astype(BF16), row(final_norm), final_norm=(l == depth - 1))
    return xt.reshape(bsz, seq, d)
```

```python
import functools
import math

import jax
import jax.numpy as jnp
from jax import lax
from jax.experimental import pallas as pl
from jax.experimental.pallas import tpu as pltpu

F32 = jnp.float32
BF16 = jnp.bfloat16

EPS = 1e-6
NEG_INF = -1e30

HEAD_DIM = 64
LANES = 128
A_HEADS = 12
A_WIDTH = A_HEADS * HEAD_DIM
A_PATTERNS = ((128, 1), (512, 4), (2048, 16))
A_BLOCK = 128
B_WINDOWS = (2, 4, 8, 16)
B_GROUP_DIM = 128
B_WIDTH = len(B_WINDOWS) * B_GROUP_DIM
B_HALO = 16
C_HEADS = 6
C_V_DIM = 2 * HEAD_DIM
C_WIDTH = C_HEADS * C_V_DIM
MAIN_COLS = 3 * A_WIDTH + 3 * C_WIDTH
QK_SCALE = HEAD_DIM ** -0.5
LOG2_E = math.log2(math.e)
BF16_ROWS = 16

VMEM_LIMIT_BYTES = 56 * 1024 * 1024


def _params(n_axes):
    return pltpu.CompilerParams(
        dimension_semantics=("arbitrary",) * n_axes,
        vmem_limit_bytes=VMEM_LIMIT_BYTES,
    )


def _rmsnorm_rows(x, g):
    ms = jnp.mean(x * x, axis=-1, keepdims=True)
    return x * lax.rsqrt(ms + EPS) * g


def _ffn_kernel(x_ref, g_ref, wg_ref, wu_ref, wd_ref, fg_ref, o_ref, h_ref, *, final_norm):
    f = pl.program_id(1)

    @pl.when(f == 0)
    def _():
        x = x_ref[...]
        h_ref[...] = _rmsnorm_rows(x, g_ref[...]).astype(BF16)
        o_ref[...] = x

    h = h_ref[...]
    gate = jnp.dot(h, wg_ref[...], preferred_element_type=F32)
    up = jnp.dot(h, wu_ref[...], preferred_element_type=F32)
    act = (gate * (1.0 / (1.0 + jnp.exp(-gate))) * up).astype(BF16)
    o_ref[...] += 0.5 * jnp.dot(act, wd_ref[...], preferred_element_type=F32)

    if final_norm:
        @pl.when(f == pl.num_programs(1) - 1)
        def _():
            o_ref[...] = _rmsnorm_rows(o_ref[...], fg_ref[...])


def _ffn(x, g, wg, wu, wd, final_g, *, final_norm, tm=512, tf=512):
    t, d = x.shape
    d_ff = wg.shape[1]
    assert t % tm == 0 and d_ff % tf == 0
    return pl.pallas_call(
        functools.partial(_ffn_kernel, final_norm=final_norm),
        grid=(t // tm, d_ff // tf),
        in_specs=[
            pl.BlockSpec((tm, d), lambda i, f: (i, 0)),
            pl.BlockSpec((1, d), lambda i, f: (0, 0)),
            pl.BlockSpec((d, tf), lambda i, f: (0, f)),
            pl.BlockSpec((d, tf), lambda i, f: (0, f)),
            pl.BlockSpec((tf, d), lambda i, f: (f, 0)),
            pl.BlockSpec((1, d), lambda i, f: (0, 0)),
        ],
        out_specs=pl.BlockSpec((tm, d), lambda i, f: (i, 0)),
        out_shape=jax.ShapeDtypeStruct((t, d), F32),
        scratch_shapes=[pltpu.VMEM((tm, d), BF16)],
        compiler_params=_params(2),
        name="ffn_final" if final_norm else "ffn",
    )(x, g, wg, wu, wd, final_g)


def _norm_matmul_kernel(x_ref, g_ref, w_ref, o_ref, h_ref):
    @pl.when(pl.program_id(1) == 0)
    def _():
        h_ref[...] = _rmsnorm_rows(x_ref[...], g_ref[...]).astype(BF16)

    o_ref[...] = jnp.dot(h_ref[...], w_ref[...], preferred_element_type=F32).astype(o_ref.dtype)


def _norm_matmul(x, g, w, out_dtype, *, tm=1024, tn=512, name):
    t, d = x.shape
    n = w.shape[1]
    tm = min(tm, t)
    assert t % tm == 0 and n % tn == 0
    return pl.pallas_call(
        _norm_matmul_kernel,
        grid=(t // tm, n // tn),
        in_specs=[
            pl.BlockSpec((tm, d), lambda i, j: (i, 0)),
            pl.BlockSpec((1, d), lambda i, j: (0, 0)),
            pl.BlockSpec((d, tn), lambda i, j: (0, j)),
        ],
        out_specs=pl.BlockSpec((tm, tn), lambda i, j: (i, j)),
        out_shape=jax.ShapeDtypeStruct((t, n), out_dtype),
        scratch_shapes=[pltpu.VMEM((tm, d), BF16)],
        compiler_params=_params(2),
        name=name,
    )(x, g, w)


def _dilated_kernel(q_ref, kp_ref, kc_ref, vp_ref, vc_ref, o_ref, lse_ref):
    blk = pl.program_id(1)
    qi = lax.broadcasted_iota(jnp.int32, (A_BLOCK, 2 * A_BLOCK), 0)
    kj = lax.broadcasted_iota(jnp.int32, (A_BLOCK, 2 * A_BLOCK), 1)
    lowest = jnp.where(blk > 0, qi, A_BLOCK)
    valid = (kj >= lowest) & (kj <= qi + A_BLOCK)
    lane = lax.broadcasted_iota(jnp.int32, (A_BLOCK, LANES), 1)
    low_half = lane < HEAD_DIM

    q_all = q_ref[0] * QK_SCALE
    for hp in range(A_HEADS // 2):
        sl = slice(hp * LANES, (hp + 1) * LANES)
        q2 = q_all[:, sl]
        k2 = jnp.concatenate([kp_ref[0, :, sl], kc_ref[0, :, sl]], axis=0)
        v2 = jnp.concatenate([vp_ref[0, :, sl], vc_ref[0, :, sl]], axis=0)
        outs, lses = [], []
        for half in range(2):
            qm = jnp.where(low_half, q2, 0) if half == 0 else jnp.where(low_half, 0, q2)
            s = lax.dot_general(qm, k2, (((1,), (1,)), ((), ())), preferred_element_type=F32)
            s = jnp.where(valid, s, NEG_INF)
            m = jnp.max(s, axis=-1, keepdims=True)
            p = jnp.exp(s - m)
            l = jnp.sum(p, axis=-1, keepdims=True)
            o = jnp.dot(p.astype(BF16), v2, preferred_element_type=F32)
            outs.append(o / l)
            lses.append(jnp.broadcast_to(m + jnp.log(l), (A_BLOCK, LANES)))
        o_ref[0, :, sl] = jnp.where(low_half, outs[0], outs[1])
        lse_ref[0, :, sl] = jnp.where(low_half, lses[0], lses[1])


def _dilated_attention(proj, bsz, seq, dilation):
    sub = seq // dilation
    assert sub % A_BLOCK == 0
    nblk = sub // A_BLOCK
    cols = proj.shape[-1]
    cpb = cols // A_WIDTH
    view = proj.reshape(bsz, sub, dilation * cols)

    def spec(which, prev):
        if prev:
            return pl.BlockSpec((1, A_BLOCK, A_WIDTH),
                                lambda b, i, r: (b, jnp.maximum(i - 1, 0), r * cpb + which))
        return pl.BlockSpec((1, A_BLOCK, A_WIDTH), lambda b, i, r: (b, i, r * cpb + which))

    out_spec = pl.BlockSpec((1, A_BLOCK, A_WIDTH), lambda b, i, r: (b, i, r))
    out_shape = jax.ShapeDtypeStruct((bsz, sub, dilation * A_WIDTH), F32)
    o, lse = pl.pallas_call(
        _dilated_kernel,
        grid=(bsz, nblk, dilation),
        in_specs=[spec(0, False), spec(1, True), spec(1, False), spec(2, True), spec(2, False)],
        out_specs=[out_spec, out_spec],
        out_shape=[out_shape, out_shape],
        compiler_params=_params(3),
        name=f"dilated_attn_d{dilation}",
    )(view, view, view, view, view)
    return o.reshape(bsz * seq, A_WIDTH), lse.reshape(bsz * seq, A_WIDTH)


def _diff_attn_kernel(q_ref, k_ref, v_ref, lq1_ref, lk1_ref, lq2_ref, lk2_ref, g_ref, o_ref,
                      qt_ref, vt_ref, s_ref, m_ref, acc_ref, *, tq, lambda_init):
    tk = tq // 2
    qb = pl.program_id(2)
    seq = v_ref.shape[1]

    @pl.when(qb == 0)
    def _():
        def transpose_rows(c, carry):
            off = pl.multiple_of(c * tq, tq)
            vt_ref[0:C_V_DIM, pl.ds(off, tq)] = (
                v_ref[0, pl.ds(off, tq), :].astype(F32).T.astype(BF16))
            return carry
        lax.fori_loop(0, seq // tq, transpose_rows, 0)
        vt_ref[C_V_DIM:, :] = jnp.ones((vt_ref.shape[0] - C_V_DIM, seq), BF16)

    qt = (q_ref[0].astype(F32) * (QK_SCALE * LOG2_E)).T.astype(BF16)
    dim = lax.broadcasted_iota(jnp.int32, (2 * HEAD_DIM, tq), 0)
    zeros = jnp.zeros_like(qt)
    qt_ref[0] = jnp.where(dim < HEAD_DIM, qt, zeros)
    qt_ref[1] = jnp.where(dim < HEAD_DIM, zeros, qt)
    m_ref[...] = jnp.full(m_ref.shape, NEG_INF, F32)
    acc_ref[...] = jnp.zeros(acc_ref.shape, F32)

    def score(slot, key_off, diag_half):
        k = k_ref[0, pl.ds(pl.multiple_of(key_off, tk), tk), :]
        for mi in range(2):
            st = jnp.dot(k, qt_ref[mi], preferred_element_type=F32)
            if diag_half is not None:
                key = lax.broadcasted_iota(jnp.int32, (tk, tq), 0) + diag_half * tk
                qry = lax.broadcasted_iota(jnp.int32, (tk, tq), 1)
                st = jnp.where(key <= qry, st, NEG_INF)
            s_ref[slot, mi] = st

    def absorb(slot, key_off):
        vt = vt_ref[:, pl.ds(pl.multiple_of(key_off, tk), tk)]
        for mi in range(2):
            m_old = m_ref[mi]
            m_new = jnp.maximum(m_old, jnp.max(s_ref[slot, mi], axis=0, keepdims=True))
            alpha = jnp.exp2(m_old - m_new)
            pt = jnp.exp2(s_ref[slot, mi] - m_new)
            m_ref[mi] = m_new
            pv = jnp.dot(vt, pt.astype(BF16), preferred_element_type=F32)
            acc_ref[mi] = alpha * acc_ref[mi] + pv

    diag = qb * tq
    score(0, diag, 0)
    score(1, diag + tk, 1)
    absorb(0, diag)

    def block(i, carry):
        off = i * tq
        score(0, off, None)
        absorb(1, jnp.where(i == 0, diag + tk, off - tk))
        score(1, off + tk, None)
        absorb(0, off)
        return carry

    lax.fori_loop(0, qb, block, 0)
    absorb(1, jnp.where(qb == 0, diag + tk, diag - tk))

    lam = (jnp.exp(jnp.sum(lq1_ref[...] * lk1_ref[...], axis=-1, keepdims=True))
           - jnp.exp(jnp.sum(lq2_ref[...] * lk2_ref[...], axis=-1, keepdims=True))
           + lambda_init)
    l1 = acc_ref[0, C_V_DIM:C_V_DIM + 1, :]
    l2 = acc_ref[1, C_V_DIM:C_V_DIM + 1, :]
    ot = acc_ref[0, 0:C_V_DIM, :] / l1 - lam * (acc_ref[1, 0:C_V_DIM, :] / l2)
    ms = jnp.mean(ot * ot, axis=0, keepdims=True)
    ot = ot * lax.rsqrt(ms + EPS) * g_ref[...] * (1.0 - lambda_init)
    o_ref[0] = ot.T.astype(o_ref.dtype)


def _diff_attention(proj, bsz, seq, lq1, lk1, lq2, lk2, subln_g, lambda_init, *, tq=1024):
    tq = min(tq, seq)
    assert seq % tq == 0
    cols = proj.shape[-1]
    view = proj.reshape(bsz, seq, cols)
    q0 = 3 * A_WIDTH // LANES
    k0 = q0 + C_HEADS
    v0 = k0 + C_HEADS
    vec = lambda: pl.BlockSpec((1, HEAD_DIM), lambda b, h, i: (0, 0))
    out = pl.pallas_call(
        functools.partial(_diff_attn_kernel, tq=tq, lambda_init=lambda_init),
        grid=(bsz, C_HEADS, seq // tq),
        in_specs=[
            pl.BlockSpec((1, tq, LANES), lambda b, h, i: (b, i, q0 + h)),
            pl.BlockSpec((1, seq, LANES), lambda b, h, i: (b, 0, k0 + h)),
            pl.BlockSpec((1, seq, LANES), lambda b, h, i: (b, 0, v0 + h)),
            vec(), vec(), vec(), vec(),
            pl.BlockSpec((C_V_DIM, 1), lambda b, h, i: (0, 0)),
        ],
        out_specs=pl.BlockSpec((1, tq, LANES), lambda b, h, i: (b, i, h)),
        out_shape=jax.ShapeDtypeStruct((bsz, seq, C_WIDTH), BF16),
        scratch_shapes=[
            pltpu.VMEM((2, C_V_DIM, tq), BF16),
            pltpu.VMEM((C_V_DIM + BF16_ROWS, seq), BF16),
            pltpu.VMEM((2, 2, tq // 2, tq), F32),
            pltpu.VMEM((2, 1, tq), F32),
            pltpu.VMEM((2, C_V_DIM + BF16_ROWS, tq), F32),
        ],
        compiler_params=_params(3),
        name="diff_attn",
    )(view, view, view, lq1, lk1, lq2, lk2, subln_g.reshape(C_V_DIM, 1))
    return out.reshape(bsz * seq, C_WIDTH)


def _mix_out_kernel(x_ref, o1_ref, o2_ref, o3_ref, l1_ref, l2_ref, l3_ref, u_ref, halo_ref,
                    c_ref, pw_ref, ps_ref, wo_ref, out_ref, *, tm, rows_per_seq):
    i = pl.program_id(0)

    l1, l2, l3 = l1_ref[...], l2_ref[...], l3_ref[...]
    mx = jnp.maximum(jnp.maximum(l1, l2), l3)
    e1, e2, e3 = jnp.exp(l1 - mx), jnp.exp(l2 - mx), jnp.exp(l3 - mx)
    den = e1 + e2 + e3
    a = (e1 / den) * o1_ref[...] + (e2 / den) * o2_ref[...] + (e3 / den) * o3_ref[...]
    acc = jnp.dot(a.astype(BF16), wo_ref[0:A_WIDTH, :], preferred_element_type=F32)

    t_in_seq = (i * tm) % rows_per_seq + lax.broadcasted_iota(jnp.int32, (tm, 1), 0)
    first = (i * tm) % rows_per_seq == 0
    halo = jnp.where(first, jnp.zeros((B_HALO, B_WIDTH), F32), halo_ref[...])
    u = u_ref[...]
    ext = jnp.concatenate([halo, u], axis=0)
    ys = []
    win = ext
    width = 1
    for g, w in enumerate(B_WINDOWS):
        while width < w:
            win = win + pltpu.roll(win, width, 0)
            width *= 2
        gs = slice(g * B_GROUP_DIM, (g + 1) * B_GROUP_DIM)
        cnt = jnp.minimum(t_in_seq + 1, w).astype(F32)
        diff = win[B_HALO:, gs] / cnt - u[:, gs]
        ys.append(jnp.dot(diff.astype(BF16), pw_ref[g], preferred_element_type=F32))
    b = jnp.concatenate(ys, axis=-1) * ps_ref[...]
    acc += jnp.dot(b.astype(BF16), wo_ref[A_WIDTH:A_WIDTH + B_WIDTH, :], preferred_element_type=F32)

    acc += jnp.dot(c_ref[...], wo_ref[A_WIDTH + B_WIDTH:, :], preferred_element_type=F32)
    out_ref[...] = x_ref[...] + acc


def _mix_out(x, a_outs, a_lses, u, c, pool_w, pool_scale, w_out, seq, *, tm=256):
    t, d = x.shape
    assert t % tm == 0 and seq % tm == 0 and tm % B_HALO == 0
    row = lambda width: pl.BlockSpec((tm, width), lambda i: (i, 0))
    halo_blocks = tm // B_HALO
    return pl.pallas_call(
        functools.partial(_mix_out_kernel, tm=tm, rows_per_seq=seq),
        grid=(t // tm,),
        in_specs=[
            row(d),
            row(A_WIDTH), row(A_WIDTH), row(A_WIDTH),
            row(A_WIDTH), row(A_WIDTH), row(A_WIDTH),
            row(B_WIDTH),
            pl.BlockSpec((B_HALO, B_WIDTH), lambda i: (jnp.maximum(i * halo_blocks - 1, 0), 0)),
            row(C_WIDTH),
            pl.BlockSpec(pool_w.shape, lambda i: (0, 0, 0)),
            pl.BlockSpec((1, B_WIDTH), lambda i: (0, 0)),
            pl.BlockSpec(w_out.shape, lambda i: (0, 0)),
        ],
        out_specs=row(d),
        out_shape=jax.ShapeDtypeStruct((t, d), F32),
        compiler_params=_params(1),
        name="mix_out",
    )(x, *a_outs, *a_lses, u, u, c, pool_w, pool_scale, w_out)


def kernel(x, ffn1_norm, ffn1_w_gate, ffn1_w_up, ffn1_w_down, mix_norm, w_in, pool_w, pool_scale,
           diff_lambda_q1, diff_lambda_k1, diff_lambda_q2, diff_lambda_k2, diff_subln, w_out,
           ffn2_norm, ffn2_w_gate, ffn2_w_up, ffn2_w_down, final_norm):
    bsz, seq, d = x.shape
    depth = w_in.shape[0]
    assert w_in.shape[2] == MAIN_COLS + B_WIDTH and d == A_WIDTH + B_WIDTH + C_WIDTH
    xt = x.reshape(bsz * seq, d)
    row = lambda v: v.reshape(1, -1).astype(F32)
    b_lo = 3 * A_WIDTH
    for l in range(depth):
        xt = _ffn(xt, row(ffn1_norm[l]), ffn1_w_gate[l].astype(BF16), ffn1_w_up[l].astype(BF16),
                  ffn1_w_down[l].astype(BF16), row(final_norm), final_norm=False)

        w_l = w_in[l]
        w_main = jnp.concatenate([w_l[:, :b_lo], w_l[:, b_lo + B_WIDTH:]], axis=1).astype(BF16)
        w_u = w_l[:, b_lo:b_lo + B_WIDTH].astype(BF16)
        g_mix = row(mix_norm[l])
        proj = _norm_matmul(xt, g_mix, w_main, BF16, name="proj_main")
        u = _norm_matmul(xt, g_mix, w_u, F32, name="proj_pool")

        a_outs, a_lses = [], []
        for window, dilation in A_PATTERNS:
            assert window // dilation == A_BLOCK
            o, lse = _dilated_attention(proj, bsz, seq, dilation)
            a_outs.append(o)
            a_lses.append(lse)

        lambda_init = 0.8 - 0.6 * math.exp(-0.3 * l)
        c = _diff_attention(proj, bsz, seq, row(diff_lambda_q1[l]), row(diff_lambda_k1[l]),
                            row(diff_lambda_q2[l]), row(diff_lambda_k2[l]), row(diff_subln[l]),
                            lambda_init)

        xt = _mix_out(xt, a_outs, a_lses, u, c, pool_w[l].astype(BF16), row(pool_scale[l]),
                      w_out[l].astype(BF16), seq)

        xt = _ffn(xt, row(ffn2_norm[l]), ffn2_w_gate[l].astype(BF16), ffn2_w_up[l].astype(BF16),
                  ffn2_w_down[l].astype(BF16), row(final_norm), final_norm=(l == depth - 1))
    return xt.reshape(bsz, seq, d)
```

```python
import functools
import math

import jax
import jax.numpy as jnp
from jax import lax
from jax.experimental import pallas as pl
from jax.experimental.pallas import tpu as pltpu

F32 = jnp.float32
BF16 = jnp.bfloat16

EPS = 1e-6
NEG_INF = -1e30

HEAD_DIM = 64
LANES = 128
A_HEADS = 12
A_WIDTH = A_HEADS * HEAD_DIM
A_PATTERNS = ((128, 1), (512, 4), (2048, 16))
A_BLOCK = 128
RESIDUES_PER_TRIP = 8
B_WINDOWS = (2, 4, 8, 16)
B_GROUP_DIM = 128
B_WIDTH = len(B_WINDOWS) * B_GROUP_DIM
B_HALO = 16
C_HEADS = 6
C_V_DIM = 2 * HEAD_DIM
C_WIDTH = C_HEADS * C_V_DIM
QK_SCALE = HEAD_DIM ** -0.5
LOG2_E = math.log2(math.e)
BF16_ROWS = 16

VMEM_LIMIT_BYTES = 56 * 1024 * 1024


def _params(n_axes):
    return pltpu.CompilerParams(
        dimension_semantics=("arbitrary",) * n_axes,
        vmem_limit_bytes=VMEM_LIMIT_BYTES,
    )


def _rmsnorm_rows(x, g):
    ms = jnp.mean(x * x, axis=-1, keepdims=True)
    return x * lax.rsqrt(ms + EPS) * g


def _ffn_kernel(x_ref, g_ref, wg_ref, wu_ref, wd_ref, fg_ref, o_ref, h_ref, *, final_norm):
    f = pl.program_id(1)

    @pl.when(f == 0)
    def _():
        x = x_ref[...]
        h_ref[...] = _rmsnorm_rows(x, g_ref[...]).astype(BF16)
        o_ref[...] = x

    h = h_ref[...]
    gate = jnp.dot(h, wg_ref[...], preferred_element_type=F32)
    up = jnp.dot(h, wu_ref[...], preferred_element_type=F32)
    act = (gate * (1.0 / (1.0 + jnp.exp(-gate))) * up).astype(BF16)
    o_ref[...] += 0.5 * jnp.dot(act, wd_ref[...], preferred_element_type=F32)

    if final_norm:
        @pl.when(f == pl.num_programs(1) - 1)
        def _():
            o_ref[...] = _rmsnorm_rows(o_ref[...], fg_ref[...])


def _ffn(x, g, wg, wu, wd, final_g, *, final_norm, tm=512, tf=512):
    t, d = x.shape
    d_ff = wg.shape[1]
    assert t % tm == 0 and d_ff % tf == 0
    return pl.pallas_call(
        functools.partial(_ffn_kernel, final_norm=final_norm),
        grid=(t // tm, d_ff // tf),
        in_specs=[
            pl.BlockSpec((tm, d), lambda i, f: (i, 0)),
            pl.BlockSpec((1, d), lambda i, f: (0, 0)),
            pl.BlockSpec((d, tf), lambda i, f: (0, f)),
            pl.BlockSpec((d, tf), lambda i, f: (0, f)),
            pl.BlockSpec((tf, d), lambda i, f: (f, 0)),
            pl.BlockSpec((1, d), lambda i, f: (0, 0)),
        ],
        out_specs=pl.BlockSpec((tm, d), lambda i, f: (i, 0)),
        out_shape=jax.ShapeDtypeStruct((t, d), F32),
        scratch_shapes=[pltpu.VMEM((tm, d), BF16)],
        compiler_params=_params(2),
        name="ffn_final" if final_norm else "ffn",
    )(x, g, wg, wu, wd, final_g)


def _norm_matmul_kernel(x_ref, g_ref, w_ref, o_ref, h_ref):
    @pl.when(pl.program_id(1) == 0)
    def _():
        h_ref[...] = _rmsnorm_rows(x_ref[...], g_ref[...]).astype(BF16)

    o_ref[...] = jnp.dot(h_ref[...], w_ref[...], preferred_element_type=F32).astype(o_ref.dtype)


def _norm_matmul(x, g, w, out_dtype, *, tm=1024, tn=512, name):
    t, d = x.shape
    n = w.shape[1]
    tm = min(tm, t)
    assert t % tm == 0 and n % tn == 0
    return pl.pallas_call(
        _norm_matmul_kernel,
        grid=(t // tm, n // tn),
        in_specs=[
            pl.BlockSpec((tm, d), lambda i, j: (i, 0)),
            pl.BlockSpec((1, d), lambda i, j: (0, 0)),
            pl.BlockSpec((d, tn), lambda i, j: (0, j)),
        ],
        out_specs=pl.BlockSpec((tm, tn), lambda i, j: (i, j)),
        out_shape=jax.ShapeDtypeStruct((t, n), out_dtype),
        scratch_shapes=[pltpu.VMEM((tm, d), BF16)],
        compiler_params=_params(2),
        name=name,
    )(x, g, w)


def _band_mask(first_block):
    qi = lax.broadcasted_iota(jnp.int32, (A_BLOCK, 2 * A_BLOCK), 0)
    kj = lax.broadcasted_iota(jnp.int32, (A_BLOCK, 2 * A_BLOCK), 1)
    lowest = jnp.where(first_block, A_BLOCK, qi)
    return (kj >= lowest) & (kj <= qi + A_BLOCK)


def _head_pair_attention(q2, k2, v2, valid, low_half):
    outs, lses = [], []
    for half in range(2):
        qm = jnp.where(low_half, q2, 0) if half == 0 else jnp.where(low_half, 0, q2)
        s = lax.dot_general(qm, k2, (((1,), (1,)), ((), ())), preferred_element_type=F32)
        s = jnp.where(valid, s, NEG_INF)
        m = jnp.max(s, axis=-1, keepdims=True)
        p = jnp.exp(s - m)
        l = jnp.sum(p, axis=-1, keepdims=True)
        o = jnp.dot(p.astype(BF16), v2, preferred_element_type=F32)
        outs.append(o / l)
        lses.append(jnp.broadcast_to(m + jnp.log(l), (A_BLOCK, LANES)))
    return jnp.where(low_half, outs[0], outs[1]), jnp.where(low_half, lses[0], lses[1])


def _dense_window_kernel(q_ref, kp_ref, kc_ref, vp_ref, vc_ref, o_ref, lse_ref):
    valid = _band_mask(pl.program_id(1) == 0)
    low_half = lax.broadcasted_iota(jnp.int32, (A_BLOCK, LANES), 1) < HEAD_DIM
    for hp in range(A_HEADS // 2):
        sl = slice(hp * LANES, (hp + 1) * LANES)
        q2 = (q_ref[0, :, sl] * QK_SCALE).astype(BF16)
        k2 = jnp.concatenate([kp_ref[0, :, sl], kc_ref[0, :, sl]], axis=0).astype(BF16)
        v2 = jnp.concatenate([vp_ref[0, :, sl], vc_ref[0, :, sl]], axis=0).astype(BF16)
        o_ref[0, :, sl], lse_ref[0, :, sl] = _head_pair_attention(q2, k2, v2, valid, low_half)


def _strided_window_kernel(q_ref, kp_ref, kc_ref, vp_ref, vc_ref, o_ref, lse_ref, *, dilation):
    valid = _band_mask(pl.program_id(1) == 0)
    low_half = lax.broadcasted_iota(jnp.int32, (A_BLOCK, LANES), 1) < HEAD_DIM

    per_trip = min(dilation, RESIDUES_PER_TRIP)

    def residues(g, carry):
        for u in range(per_trip):
            rows = pl.ds(g * per_trip + u, A_BLOCK, stride=dilation)
            q2 = (q_ref[0, rows, :] * QK_SCALE).astype(BF16)
            k2 = jnp.concatenate([kp_ref[0, rows, :], kc_ref[0, rows, :]], axis=0).astype(BF16)
            v2 = jnp.concatenate([vp_ref[0, rows, :], vc_ref[0, rows, :]], axis=0).astype(BF16)
            o_ref[0, rows, :], lse_ref[0, rows, :] = _head_pair_attention(
                q2, k2, v2, valid, low_half)
        return carry

    assert dilation % per_trip == 0
    lax.fori_loop(0, dilation // per_trip, residues, 0)


def _dilated_attention(a_qkv, bsz, seq, dilation):
    rows = A_BLOCK * dilation
    assert seq % rows == 0
    view = a_qkv.reshape(bsz, seq, 3 * A_WIDTH)
    out_shape = jax.ShapeDtypeStruct((bsz, seq, A_WIDTH), F32)
    if dilation == 1:
        width, per_part = A_WIDTH, 1
        grid = (bsz, seq // rows)
        body = _dense_window_kernel
        cur = lambda part: pl.BlockSpec((1, rows, width), lambda b, i: (b, i, part))
        prev = lambda part: pl.BlockSpec((1, rows, width),
                                         lambda b, i: (b, jnp.maximum(i - 1, 0), part))
        out_spec = pl.BlockSpec((1, rows, width), lambda b, i: (b, i, 0))
    else:
        width, per_part = LANES, A_WIDTH // LANES
        grid = (bsz, seq // rows, per_part)
        body = functools.partial(_strided_window_kernel, dilation=dilation)
        cur = lambda part: pl.BlockSpec((1, rows, width),
                                        lambda b, i, hp: (b, i, part * per_part + hp))
        prev = lambda part: pl.BlockSpec(
            (1, rows, width), lambda b, i, hp: (b, jnp.maximum(i - 1, 0), part * per_part + hp))
        out_spec = pl.BlockSpec((1, rows, width), lambda b, i, hp: (b, i, hp))
    o, lse = pl.pallas_call(
        body,
        grid=grid,
        in_specs=[cur(0), prev(1), cur(1), prev(2), cur(2)],
        out_specs=[out_spec, out_spec],
        out_shape=[out_shape, out_shape],
        compiler_params=_params(len(grid)),
        name=f"dilated_attn_d{dilation}",
    )(view, view, view, view, view)
    return o.reshape(bsz * seq, A_WIDTH), lse.reshape(bsz * seq, A_WIDTH)


def _diff_attn_kernel(q_ref, k_ref, v_ref, lq1_ref, lk1_ref, lq2_ref, lk2_ref, g_ref, o_ref,
                      qt_ref, vt_ref, s_ref, m_ref, acc_ref, *, tq, lambda_init):
    tk = tq // 2
    qb = pl.program_id(2)
    seq = v_ref.shape[1]

    @pl.when(qb == 0)
    def _():
        def transpose_rows(c, carry):
            off = pl.multiple_of(c * tq, tq)
            vt_ref[0:C_V_DIM, pl.ds(off, tq)] = (
                v_ref[0, pl.ds(off, tq), :].astype(F32).T.astype(BF16))
            return carry
        lax.fori_loop(0, seq // tq, transpose_rows, 0)
        vt_ref[C_V_DIM:, :] = jnp.ones((vt_ref.shape[0] - C_V_DIM, seq), BF16)

    qt = (q_ref[0].astype(F32) * (QK_SCALE * LOG2_E)).T.astype(BF16)
    dim = lax.broadcasted_iota(jnp.int32, (2 * HEAD_DIM, tq), 0)
    zeros = jnp.zeros_like(qt)
    qt_ref[0] = jnp.where(dim < HEAD_DIM, qt, zeros)
    qt_ref[1] = jnp.where(dim < HEAD_DIM, zeros, qt)
    m_ref[...] = jnp.full(m_ref.shape, NEG_INF, F32)
    acc_ref[...] = jnp.zeros(acc_ref.shape, F32)

    def score(slot, key_off, diag_half):
        k = k_ref[0, pl.ds(pl.multiple_of(key_off, tk), tk), :]
        for mi in range(2):
            st = jnp.dot(k, qt_ref[mi], preferred_element_type=F32)
            if diag_half is not None:
                key = lax.broadcasted_iota(jnp.int32, (tk, tq), 0) + diag_half * tk
                qry = lax.broadcasted_iota(jnp.int32, (tk, tq), 1)
                st = jnp.where(key <= qry, st, NEG_INF)
            s_ref[slot, mi] = st

    def absorb(slot, key_off):
        vt = vt_ref[:, pl.ds(pl.multiple_of(key_off, tk), tk)]
        for mi in range(2):
            m_old = m_ref[mi]
            m_new = jnp.maximum(m_old, jnp.max(s_ref[slot, mi], axis=0, keepdims=True))
            alpha = jnp.exp2(m_old - m_new)
            pt = jnp.exp2(s_ref[slot, mi] - m_new)
            m_ref[mi] = m_new
            pv = jnp.dot(vt, pt.astype(BF16), preferred_element_type=F32)
            acc_ref[mi] = alpha * acc_ref[mi] + pv

    diag = qb * tq
    score(0, diag, 0)
    score(1, diag + tk, 1)
    absorb(0, diag)

    def block(i, carry):
        off = i * tq
        score(0, off, None)
        absorb(1, jnp.where(i == 0, diag + tk, off - tk))
        score(1, off + tk, None)
        absorb(0, off)
        return carry

    lax.fori_loop(0, qb, block, 0)
    absorb(1, jnp.where(qb == 0, diag + tk, diag - tk))

    lam = (jnp.exp(jnp.sum(lq1_ref[...] * lk1_ref[...], axis=-1, keepdims=True))
           - jnp.exp(jnp.sum(lq2_ref[...] * lk2_ref[...], axis=-1, keepdims=True))
           + lambda_init)
    l1 = acc_ref[0, C_V_DIM:C_V_DIM + 1, :]
    l2 = acc_ref[1, C_V_DIM:C_V_DIM + 1, :]
    ot = acc_ref[0, 0:C_V_DIM, :] / l1 - lam * (acc_ref[1, 0:C_V_DIM, :] / l2)
    ms = jnp.mean(ot * ot, axis=0, keepdims=True)
    ot = ot * lax.rsqrt(ms + EPS) * g_ref[...] * (1.0 - lambda_init)
    o_ref[0] = ot.T.astype(o_ref.dtype)


def _diff_attention(proj, bsz, seq, lq1, lk1, lq2, lk2, subln_g, lambda_init, *, tq=1024):
    tq = min(tq, seq)
    assert seq % tq == 0
    cols = proj.shape[-1]
    view = proj.reshape(bsz, seq, cols)
    q0 = 0
    k0 = q0 + C_HEADS
    v0 = k0 + C_HEADS
    vec = lambda: pl.BlockSpec((1, HEAD_DIM), lambda b, h, i: (0, 0))
    out = pl.pallas_call(
        functools.partial(_diff_attn_kernel, tq=tq, lambda_init=lambda_init),
        grid=(bsz, C_HEADS, seq // tq),
        in_specs=[
            pl.BlockSpec((1, tq, LANES), lambda b, h, i: (b, i, q0 + h)),
            pl.BlockSpec((1, seq, LANES), lambda b, h, i: (b, 0, k0 + h)),
            pl.BlockSpec((1, seq, LANES), lambda b, h, i: (b, 0, v0 + h)),
            vec(), vec(), vec(), vec(),
            pl.BlockSpec((C_V_DIM, 1), lambda b, h, i: (0, 0)),
        ],
        out_specs=pl.BlockSpec((1, tq, LANES), lambda b, h, i: (b, i, h)),
        out_shape=jax.ShapeDtypeStruct((bsz, seq, C_WIDTH), BF16),
        scratch_shapes=[
            pltpu.VMEM((2, C_V_DIM, tq), BF16),
            pltpu.VMEM((C_V_DIM + BF16_ROWS, seq), BF16),
            pltpu.VMEM((2, 2, tq // 2, tq), F32),
            pltpu.VMEM((2, 1, tq), F32),
            pltpu.VMEM((2, C_V_DIM + BF16_ROWS, tq), F32),
        ],
        compiler_params=_params(3),
        name="diff_attn",
    )(view, view, view, lq1, lk1, lq2, lk2, subln_g.reshape(C_V_DIM, 1))
    return out.reshape(bsz * seq, C_WIDTH)


def _mix_out_kernel(x_ref, o1_ref, o2_ref, o3_ref, l1_ref, l2_ref, l3_ref, u_ref, halo_ref,
                    c_ref, pw_ref, ps_ref, wo_ref, out_ref, *, tm, rows_per_seq):
    i = pl.program_id(0)

    l1, l2, l3 = l1_ref[...], l2_ref[...], l3_ref[...]
    mx = jnp.maximum(jnp.maximum(l1, l2), l3)
    e1, e2, e3 = jnp.exp(l1 - mx), jnp.exp(l2 - mx), jnp.exp(l3 - mx)
    den = e1 + e2 + e3
    a = (e1 / den) * o1_ref[...] + (e2 / den) * o2_ref[...] + (e3 / den) * o3_ref[...]
    acc = jnp.dot(a.astype(BF16), wo_ref[0:A_WIDTH, :], preferred_element_type=F32)

    t_in_seq = (i * tm) % rows_per_seq + lax.broadcasted_iota(jnp.int32, (tm, 1), 0)
    first = (i * tm) % rows_per_seq == 0
    halo = jnp.where(first, jnp.zeros((B_HALO, B_WIDTH), F32), halo_ref[...])
    u = u_ref[...]
    ext = jnp.concatenate([halo, u], axis=0)
    ys = []
    win = ext
    width = 1
    for g, w in enumerate(B_WINDOWS):
        while width < w:
            win = win + pltpu.roll(win, width, 0)
            width *= 2
        gs = slice(g * B_GROUP_DIM, (g + 1) * B_GROUP_DIM)
        cnt = jnp.minimum(t_in_seq + 1, w).astype(F32)
        diff = win[B_HALO:, gs] / cnt - u[:, gs]
        ys.append(jnp.dot(diff.astype(BF16), pw_ref[g], preferred_element_type=F32))
    b = jnp.concatenate(ys, axis=-1) * ps_ref[...]
    acc += jnp.dot(b.astype(BF16), wo_ref[A_WIDTH:A_WIDTH + B_WIDTH, :], preferred_element_type=F32)

    acc += jnp.dot(c_ref[...], wo_ref[A_WIDTH + B_WIDTH:, :], preferred_element_type=F32)
    out_ref[...] = x_ref[...] + acc


def _mix_out(x, a_outs, a_lses, u, c, pool_w, pool_scale, w_out, seq, *, tm=256):
    t, d = x.shape
    assert t % tm == 0 and seq % tm == 0 and tm % B_HALO == 0
    row = lambda width: pl.BlockSpec((tm, width), lambda i: (i, 0))
    halo_blocks = tm // B_HALO
    return pl.pallas_call(
        functools.partial(_mix_out_kernel, tm=tm, rows_per_seq=seq),
        grid=(t // tm,),
        in_specs=[
            row(d),
            row(A_WIDTH), row(A_WIDTH), row(A_WIDTH),
            row(A_WIDTH), row(A_WIDTH), row(A_WIDTH),
            row(B_WIDTH),
            pl.BlockSpec((B_HALO, B_WIDTH), lambda i: (jnp.maximum(i * halo_blocks - 1, 0), 0)),
            row(C_WIDTH),
            pl.BlockSpec(pool_w.shape, lambda i: (0, 0, 0)),
            pl.BlockSpec((1, B_WIDTH), lambda i: (0, 0)),
            pl.BlockSpec(w_out.shape, lambda i: (0, 0)),
        ],
        out_specs=row(d),
        out_shape=jax.ShapeDtypeStruct((t, d), F32),
        compiler_params=_params(1),
        name="mix_out",
    )(x, *a_outs, *a_lses, u, u, c, pool_w, pool_scale, w_out)


def kernel(x, ffn1_norm, ffn1_w_gate, ffn1_w_up, ffn1_w_down, mix_norm, w_in, pool_w, pool_scale,
           diff_lambda_q1, diff_lambda_k1, diff_lambda_q2, diff_lambda_k2, diff_subln, w_out,
           ffn2_norm, ffn2_w_gate, ffn2_w_up, ffn2_w_down, final_norm):
    bsz, seq, d = x.shape
    depth = w_in.shape[0]
    b_lo, c_lo = 3 * A_WIDTH, 3 * A_WIDTH + B_WIDTH
    assert w_in.shape[2] == c_lo + 3 * C_WIDTH and d == A_WIDTH + B_WIDTH + C_WIDTH
    xt = x.reshape(bsz * seq, d)
    row = lambda v: v.reshape(1, -1).astype(F32)
    for l in range(depth):
        xt = _ffn(xt, row(ffn1_norm[l]), ffn1_w_gate[l].astype(BF16), ffn1_w_up[l].astype(BF16),
                  ffn1_w_down[l].astype(BF16), row(final_norm), final_norm=False)

        w_l = w_in[l].astype(BF16)
        g_mix = row(mix_norm[l])
        a_qkv = _norm_matmul(xt, g_mix, w_l[:, :b_lo], F32, tn=A_WIDTH, name="proj_a")
        u = _norm_matmul(xt, g_mix, w_l[:, b_lo:c_lo], F32, tn=B_WIDTH, name="proj_pool")
        c_qkv = _norm_matmul(xt, g_mix, w_l[:, c_lo:], BF16, tn=C_WIDTH, name="proj_c")

        a_outs, a_lses = [], []
        for window, dilation in A_PATTERNS:
            assert window // dilation == A_BLOCK
            o, lse = _dilated_attention(a_qkv, bsz, seq, dilation)
            a_outs.append(o)
            a_lses.append(lse)

        lambda_init = 0.8 - 0.6 * math.exp(-0.3 * l)
        c = _diff_attention(c_qkv, bsz, seq, row(diff_lambda_q1[l]), row(diff_lambda_k1[l]),
                            row(diff_lambda_q2[l]), row(diff_lambda_k2[l]), row(diff_subln[l]),
                            lambda_init)

        xt = _mix_out(xt, a_outs, a_lses, u, c, pool_w[l].astype(BF16), row(pool_scale[l]),
                      w_out[l].astype(BF16), seq)

        xt = _ffn(xt, row(ffn2_norm[l]), ffn2_w_gate[l].astype(BF16), ffn2_w_up[l].astype(BF16),
                  ffn2_w_down[l].astype(BF16), row(final_norm), final_norm=(l == depth - 1))
    return xt.reshape(bsz, seq, d)
```

```python
import functools
import math

import jax
import jax.numpy as jnp
from jax import lax
from jax.experimental import pallas as pl
from jax.experimental.pallas import tpu as pltpu

F32 = jnp.float32
BF16 = jnp.bfloat16

EPS = 1e-6
NEG_INF = -1e30

HEAD_DIM = 64
LANES = 128
A_HEADS = 12
A_WIDTH = A_HEADS * HEAD_DIM
A_PATTERNS = ((128, 1), (512, 4), (2048, 16))
A_BLOCK = 128
A_SUPER = A_BLOCK * max(d for _, d in A_PATTERNS)
TASKS_PER_TRIP = 8
MIX_ROWS = 256
B_WINDOWS = (2, 4, 8, 16)
B_GROUP_DIM = 128
B_WIDTH = len(B_WINDOWS) * B_GROUP_DIM
B_HALO = 16
C_HEADS = 6
C_V_DIM = 2 * HEAD_DIM
C_WIDTH = C_HEADS * C_V_DIM
QK_SCALE = HEAD_DIM ** -0.5
LOG2_E = math.log2(math.e)
BF16_ROWS = 16

VMEM_LIMIT_BYTES = 56 * 1024 * 1024


def _params(n_axes):
    return pltpu.CompilerParams(
        dimension_semantics=("arbitrary",) * n_axes,
        vmem_limit_bytes=VMEM_LIMIT_BYTES,
    )


def _rmsnorm_rows(x, g):
    ms = jnp.mean(x * x, axis=-1, keepdims=True)
    return x * lax.rsqrt(ms + EPS) * g


def _ffn_kernel(x_ref, g_ref, wg_ref, wu_ref, wd_ref, fg_ref, o_ref, h_ref, *, final_norm):
    f = pl.program_id(1)

    @pl.when(f == 0)
    def _():
        x = x_ref[...]
        h_ref[...] = _rmsnorm_rows(x, g_ref[...]).astype(BF16)
        o_ref[...] = x

    h = h_ref[...]
    gate = jnp.dot(h, wg_ref[...], preferred_element_type=F32)
    up = jnp.dot(h, wu_ref[...], preferred_element_type=F32)
    act = (gate * (1.0 / (1.0 + jnp.exp(-gate))) * up).astype(BF16)
    o_ref[...] += 0.5 * jnp.dot(act, wd_ref[...], preferred_element_type=F32)

    if final_norm:
        @pl.when(f == pl.num_programs(1) - 1)
        def _():
            o_ref[...] = _rmsnorm_rows(o_ref[...], fg_ref[...])


def _ffn(x, g, wg, wu, wd, final_g, *, final_norm, tm=512, tf=512):
    t, d = x.shape
    d_ff = wg.shape[1]
    assert t % tm == 0 and d_ff % tf == 0
    return pl.pallas_call(
        functools.partial(_ffn_kernel, final_norm=final_norm),
        grid=(t // tm, d_ff // tf),
        in_specs=[
            pl.BlockSpec((tm, d), lambda i, f: (i, 0)),
            pl.BlockSpec((1, d), lambda i, f: (0, 0)),
            pl.BlockSpec((d, tf), lambda i, f: (0, f)),
            pl.BlockSpec((d, tf), lambda i, f: (0, f)),
            pl.BlockSpec((tf, d), lambda i, f: (f, 0)),
            pl.BlockSpec((1, d), lambda i, f: (0, 0)),
        ],
        out_specs=pl.BlockSpec((tm, d), lambda i, f: (i, 0)),
        out_shape=jax.ShapeDtypeStruct((t, d), F32),
        scratch_shapes=[pltpu.VMEM((tm, d), BF16)],
        compiler_params=_params(2),
        name="ffn_final" if final_norm else "ffn",
    )(x, g, wg, wu, wd, final_g)


PROJ_TILE = A_WIDTH
PROJ_A_TILES = 3
PROJ_C_TILES = 3


def _mix_projection_kernel(x_ref, g_ref, w_ref, a_ref, c_ref, u_ref, h_ref):
    j = pl.program_id(1)

    @pl.when(j == 0)
    def _():
        h_ref[...] = _rmsnorm_rows(x_ref[...], g_ref[...]).astype(BF16)

    res = jnp.dot(h_ref[...], w_ref[...], preferred_element_type=F32)

    @pl.when(j < PROJ_A_TILES)
    def _():
        a_ref[...] = res

    @pl.when((j >= PROJ_A_TILES) & (j < PROJ_A_TILES + PROJ_C_TILES))
    def _():
        c_ref[...] = res.astype(BF16)

    @pl.when(j == PROJ_A_TILES + PROJ_C_TILES)
    def _():
        u_ref[...] = res


def _mix_projection(x, g, w, *, tm=1024):
    t, d = x.shape
    tm = min(tm, t)
    n_tiles = PROJ_A_TILES + PROJ_C_TILES + 1
    assert t % tm == 0 and w.shape[1] == n_tiles * PROJ_TILE
    last_a, first_c = PROJ_A_TILES - 1, PROJ_A_TILES
    return pl.pallas_call(
        _mix_projection_kernel,
        grid=(t // tm, n_tiles),
        in_specs=[
            pl.BlockSpec((tm, d), lambda i, j: (i, 0)),
            pl.BlockSpec((1, d), lambda i, j: (0, 0)),
            pl.BlockSpec((d, PROJ_TILE), lambda i, j: (0, j)),
        ],
        out_specs=[
            pl.BlockSpec((tm, PROJ_TILE), lambda i, j: (i, jnp.minimum(j, last_a))),
            pl.BlockSpec((tm, PROJ_TILE),
                         lambda i, j: (i, jnp.clip(j - first_c, 0, PROJ_C_TILES - 1))),
            pl.BlockSpec((tm, PROJ_TILE), lambda i, j: (i, 0)),
        ],
        out_shape=[
            jax.ShapeDtypeStruct((t, PROJ_A_TILES * PROJ_TILE), F32),
            jax.ShapeDtypeStruct((t, PROJ_C_TILES * PROJ_TILE), BF16),
            jax.ShapeDtypeStruct((t, PROJ_TILE), F32),
        ],
        scratch_shapes=[pltpu.VMEM((tm, d), BF16)],
        compiler_params=_params(2),
        name="mix_proj",
    )(x, g, w)


def _head_pair_attention(q2, k2, v2, valid, low_half):
    outs, lses = [], []
    for half in range(2):
        qm = jnp.where(low_half, q2, 0) if half == 0 else jnp.where(low_half, 0, q2)
        s = lax.dot_general(qm, k2, (((1,), (1,)), ((), ())), preferred_element_type=F32)
        s = jnp.where(valid, s, NEG_INF)
        m = jnp.max(s, axis=-1, keepdims=True)
        p = jnp.exp(s - m)
        l = jnp.sum(p, axis=-1, keepdims=True)
        o = jnp.dot(p.astype(BF16), v2, preferred_element_type=F32)
        outs.append(o / l)
        lses.append(jnp.broadcast_to(m + jnp.log(l), (A_BLOCK, LANES)))
    return jnp.where(low_half, outs[0], outs[1]), jnp.where(low_half, lses[0], lses[1])


def _window_mixture_kernel(q_ref, k_ref, v_ref, a_ref, kk_ref, vv_ref, o_ref, lse_ref):
    first = pl.program_id(2) == 0

    @pl.when(first)
    def _():
        kk_ref[0:A_SUPER, :] = jnp.zeros((A_SUPER, LANES), F32)
        vv_ref[0:A_SUPER, :] = jnp.zeros((A_SUPER, LANES), F32)

    kk_ref[A_SUPER:, :] = k_ref[0]
    vv_ref[A_SUPER:, :] = v_ref[0]

    qi = lax.broadcasted_iota(jnp.int32, (A_BLOCK, 2 * A_BLOCK), 0)
    kj = lax.broadcasted_iota(jnp.int32, (A_BLOCK, 2 * A_BLOCK), 1)
    band = kj <= qi + A_BLOCK
    low_half = lax.broadcasted_iota(jnp.int32, (A_BLOCK, LANES), 1) < HEAD_DIM
    n_tasks = A_SUPER // A_BLOCK

    for p, (_, dilation) in enumerate(A_PATTERNS):
        def tasks(g, carry, p=p, dilation=dilation):
            for u in range(TASKS_PER_TRIP):
                t = g * TASKS_PER_TRIP + u
                sub, r = t // dilation, t % dilation
                start = sub * (A_BLOCK * dilation) + r
                lowest = jnp.where(first & (sub == 0), A_BLOCK, qi)
                valid = band & (kj >= lowest)
                q_rows = pl.ds(start, A_BLOCK, stride=dilation)
                kv_rows = pl.ds(A_SUPER + start - A_BLOCK * dilation, 2 * A_BLOCK, stride=dilation)
                q2 = (q_ref[0, q_rows, :] * QK_SCALE).astype(BF16)
                k2 = kk_ref[kv_rows, :].astype(BF16)
                v2 = vv_ref[kv_rows, :].astype(BF16)
                o_ref[p, q_rows, :], lse_ref[p, q_rows, :] = _head_pair_attention(
                    q2, k2, v2, valid, low_half)
            return carry

        lax.fori_loop(0, n_tasks // TASKS_PER_TRIP, tasks, 0)

    kk_ref[0:A_SUPER, :] = k_ref[0]
    vv_ref[0:A_SUPER, :] = v_ref[0]

    def mixture(c, carry):
        rows = pl.ds(pl.multiple_of(c * MIX_ROWS, MIX_ROWS), MIX_ROWS)
        lses = [lse_ref[p, rows, :] for p in range(len(A_PATTERNS))]
        mx = functools.reduce(jnp.maximum, lses)
        es = [jnp.exp(l - mx) for l in lses]
        den = functools.reduce(lambda x, y: x + y, es)
        a = functools.reduce(lambda x, y: x + y,
                             [(e / den) * o_ref[p, rows, :] for p, e in enumerate(es)])
        a_ref[0, rows, :] = a.astype(a_ref.dtype)
        return carry

    lax.fori_loop(0, A_SUPER // MIX_ROWS, mixture, 0)


def _window_mixture(a_qkv, bsz, seq):
    assert seq % A_SUPER == 0 and all(w // d == A_BLOCK for w, d in A_PATTERNS)
    view = a_qkv.reshape(bsz, seq, 3 * A_WIDTH)
    per_part = A_WIDTH // LANES
    part = lambda p: pl.BlockSpec((1, A_SUPER, LANES), lambda b, hp, i: (b, i, p * per_part + hp))
    n_pat = len(A_PATTERNS)
    out = pl.pallas_call(
        _window_mixture_kernel,
        grid=(bsz, per_part, seq // A_SUPER),
        in_specs=[part(0), part(1), part(2)],
        out_specs=part(0),
        out_shape=jax.ShapeDtypeStruct((bsz, seq, A_WIDTH), BF16),
        scratch_shapes=[
            pltpu.VMEM((2 * A_SUPER, LANES), F32),
            pltpu.VMEM((2 * A_SUPER, LANES), F32),
            pltpu.VMEM((n_pat, A_SUPER, LANES), F32),
            pltpu.VMEM((n_pat, A_SUPER, LANES), F32),
        ],
        compiler_params=_params(3),
        name="window_mixture",
    )(view, view, view)
    return out.reshape(bsz * seq, A_WIDTH)


def _diff_attn_kernel(q_ref, k_ref, v_ref, lq1_ref, lk1_ref, lq2_ref, lk2_ref, g_ref, o_ref,
                      qt_ref, vt_ref, s_ref, m_ref, acc_ref, *, tq, tk, lambda_init):
    n_chunks = tq // tk
    qb = pl.program_id(2)
    seq = v_ref.shape[1]

    @pl.when(qb == 0)
    def _():
        def transpose_rows(c, carry):
            off = pl.multiple_of(c * tq, tq)
            vt_ref[0:C_V_DIM, pl.ds(off, tq)] = (
                v_ref[0, pl.ds(off, tq), :].astype(F32).T.astype(BF16))
            return carry
        lax.fori_loop(0, seq // tq, transpose_rows, 0)
        vt_ref[C_V_DIM:, :] = jnp.ones((vt_ref.shape[0] - C_V_DIM, seq), BF16)

    qt = (q_ref[0].astype(F32) * (QK_SCALE * LOG2_E)).T.astype(BF16)
    dim = lax.broadcasted_iota(jnp.int32, (2 * HEAD_DIM, tq), 0)
    zeros = jnp.zeros_like(qt)
    qt_ref[0] = jnp.where(dim < HEAD_DIM, qt, zeros)
    qt_ref[1] = jnp.where(dim < HEAD_DIM, zeros, qt)
    m_ref[...] = jnp.full(m_ref.shape, NEG_INF, F32)
    acc_ref[...] = jnp.zeros(acc_ref.shape, F32)

    def score(slot, key_off, diag_chunk):
        k = k_ref[0, pl.ds(pl.multiple_of(key_off, tk), tk), :]
        for mi in range(2):
            st = jnp.dot(k, qt_ref[mi], preferred_element_type=F32)
            if diag_chunk is not None:
                key = lax.broadcasted_iota(jnp.int32, (tk, tq), 0) + diag_chunk * tk
                qry = lax.broadcasted_iota(jnp.int32, (tk, tq), 1)
                st = jnp.where(key <= qry, st, NEG_INF)
            s_ref[slot, mi] = st

    def absorb(slot, key_off):
        vt = vt_ref[:, pl.ds(pl.multiple_of(key_off, tk), tk)]
        for mi in range(2):
            m_old = m_ref[mi]
            m_new = jnp.maximum(m_old, jnp.max(s_ref[slot, mi], axis=0, keepdims=True))
            alpha = jnp.exp2(m_old - m_new)
            pt = jnp.exp2(s_ref[slot, mi] - m_new)
            m_ref[mi] = m_new
            pv = jnp.dot(vt, pt.astype(BF16), preferred_element_type=F32)
            acc_ref[mi] = alpha * acc_ref[mi] + pv

    diag = qb * tq
    last = n_chunks - 1
    score(0, diag, 0)
    for c in range(1, n_chunks):
        score(c % 2, diag + c * tk, c)
        absorb((c - 1) % 2, diag + (c - 1) * tk)

    def block(i, carry):
        off = i * tq
        score(0, off, None)
        absorb(1, jnp.where(i == 0, diag, off - tq) + last * tk)
        for c in range(1, n_chunks):
            score(c % 2, off + c * tk, None)
            absorb((c - 1) % 2, off + (c - 1) * tk)
        return carry

    lax.fori_loop(0, qb, block, 0)
    absorb(1, jnp.where(qb == 0, diag, diag - tq) + last * tk)

    lam = (jnp.exp(jnp.sum(lq1_ref[...] * lk1_ref[...], axis=-1, keepdims=True))
           - jnp.exp(jnp.sum(lq2_ref[...] * lk2_ref[...], axis=-1, keepdims=True))
           + lambda_init)
    l1 = acc_ref[0, C_V_DIM:C_V_DIM + 1, :]
    l2 = acc_ref[1, C_V_DIM:C_V_DIM + 1, :]
    ot = acc_ref[0, 0:C_V_DIM, :] / l1 - lam * (acc_ref[1, 0:C_V_DIM, :] / l2)
    ms = jnp.mean(ot * ot, axis=0, keepdims=True)
    ot = ot * lax.rsqrt(ms + EPS) * g_ref[...] * (1.0 - lambda_init)
    o_ref[0] = ot.T.astype(o_ref.dtype)


def _diff_attention(proj, bsz, seq, lq1, lk1, lq2, lk2, subln_g, lambda_init, *, tq=1024, tk=512):
    tq = min(tq, seq)
    assert seq % tq == 0 and tq % (2 * tk) == 0
    cols = proj.shape[-1]
    view = proj.reshape(bsz, seq, cols)
    q0 = 0
    k0 = q0 + C_HEADS
    v0 = k0 + C_HEADS
    vec = lambda: pl.BlockSpec((1, HEAD_DIM), lambda b, h, i: (0, 0))
    out = pl.pallas_call(
        functools.partial(_diff_attn_kernel, tq=tq, tk=tk, lambda_init=lambda_init),
        grid=(bsz, C_HEADS, seq // tq),
        in_specs=[
            pl.BlockSpec((1, tq, LANES), lambda b, h, i: (b, i, q0 + h)),
            pl.BlockSpec((1, seq, LANES), lambda b, h, i: (b, 0, k0 + h)),
            pl.BlockSpec((1, seq, LANES), lambda b, h, i: (b, 0, v0 + h)),
            vec(), vec(), vec(), vec(),
            pl.BlockSpec((C_V_DIM, 1), lambda b, h, i: (0, 0)),
        ],
        out_specs=pl.BlockSpec((1, tq, LANES), lambda b, h, i: (b, i, h)),
        out_shape=jax.ShapeDtypeStruct((bsz, seq, C_WIDTH), BF16),
        scratch_shapes=[
            pltpu.VMEM((2, C_V_DIM, tq), BF16),
            pltpu.VMEM((C_V_DIM + BF16_ROWS, seq), BF16),
            pltpu.VMEM((2, 2, tk, tq), F32),
            pltpu.VMEM((2, 1, tq), F32),
            pltpu.VMEM((2, C_V_DIM + BF16_ROWS, tq), F32),
        ],
        compiler_params=_params(3),
        name="diff_attn",
    )(view, view, view, lq1, lk1, lq2, lk2, subln_g.reshape(C_V_DIM, 1))
    return out.reshape(bsz * seq, C_WIDTH)


def _mix_out_kernel(x_ref, a_ref, u_ref, halo_ref, c_ref, pw_ref, ps_ref, wo_ref, out_ref,
                    *, tm, rows_per_seq):
    i = pl.program_id(0)

    acc = jnp.dot(a_ref[...], wo_ref[0:A_WIDTH, :], preferred_element_type=F32)

    t_in_seq = (i * tm) % rows_per_seq + lax.broadcasted_iota(jnp.int32, (tm, 1), 0)
    first = (i * tm) % rows_per_seq == 0
    halo = jnp.where(first, jnp.zeros((B_HALO, B_WIDTH), F32), halo_ref[...])
    u = u_ref[...]
    ext = jnp.concatenate([halo, u], axis=0)
    ys = []
    win = ext
    width = 1
    for g, w in enumerate(B_WINDOWS):
        while width < w:
            win = win + pltpu.roll(win, width, 0)
            width *= 2
        gs = slice(g * B_GROUP_DIM, (g + 1) * B_GROUP_DIM)
        cnt = jnp.minimum(t_in_seq + 1, w).astype(F32)
        diff = win[B_HALO:, gs] / cnt - u[:, gs]
        ys.append(jnp.dot(diff.astype(BF16), pw_ref[g], preferred_element_type=F32))
    b = jnp.concatenate(ys, axis=-1) * ps_ref[...]
    acc += jnp.dot(b.astype(BF16), wo_ref[A_WIDTH:A_WIDTH + B_WIDTH, :], preferred_element_type=F32)

    acc += jnp.dot(c_ref[...], wo_ref[A_WIDTH + B_WIDTH:, :], preferred_element_type=F32)
    out_ref[...] = x_ref[...] + acc


def _mix_out(x, a, u, c, pool_w, pool_scale, w_out, seq, *, tm=256):
    t, d = x.shape
    assert t % tm == 0 and seq % tm == 0 and tm % B_HALO == 0
    row = lambda width: pl.BlockSpec((tm, width), lambda i: (i, 0))
    halo_blocks = tm // B_HALO
    return pl.pallas_call(
        functools.partial(_mix_out_kernel, tm=tm, rows_per_seq=seq),
        grid=(t // tm,),
        in_specs=[
            row(d),
            row(A_WIDTH),
            row(B_WIDTH),
            pl.BlockSpec((B_HALO, B_WIDTH), lambda i: (jnp.maximum(i * halo_blocks - 1, 0), 0)),
            row(C_WIDTH),
            pl.BlockSpec(pool_w.shape, lambda i: (0, 0, 0)),
            pl.BlockSpec((1, B_WIDTH), lambda i: (0, 0)),
            pl.BlockSpec(w_out.shape, lambda i: (0, 0)),
        ],
        out_specs=row(d),
        out_shape=jax.ShapeDtypeStruct((t, d), F32),
        compiler_params=_params(1),
        name="mix_out",
    )(x, a, u, u, c, pool_w, pool_scale, w_out)


def kernel(x, ffn1_norm, ffn1_w_gate, ffn1_w_up, ffn1_w_down, mix_norm, w_in, pool_w, pool_scale,
           diff_lambda_q1, diff_lambda_k1, diff_lambda_q2, diff_lambda_k2, diff_subln, w_out,
           ffn2_norm, ffn2_w_gate, ffn2_w_up, ffn2_w_down, final_norm):
    bsz, seq, d = x.shape
    depth = w_in.shape[0]
    b_lo, c_lo = 3 * A_WIDTH, 3 * A_WIDTH + B_WIDTH
    assert w_in.shape[2] == c_lo + 3 * C_WIDTH and d == A_WIDTH + B_WIDTH + C_WIDTH
    xt = x.reshape(bsz * seq, d)
    row = lambda v: v.reshape(1, -1).astype(F32)
    for l in range(depth):
        xt = _ffn(xt, row(ffn1_norm[l]), ffn1_w_gate[l].astype(BF16), ffn1_w_up[l].astype(BF16),
                  ffn1_w_down[l].astype(BF16), row(final_norm), final_norm=False)

        w_l = w_in[l].astype(BF16)
        w_mix = jnp.concatenate([w_l[:, :b_lo], w_l[:, c_lo:], w_l[:, b_lo:c_lo],
                                 jnp.zeros((d, PROJ_TILE - B_WIDTH), BF16)], axis=1)
        a_qkv, c_qkv, u = _mix_projection(xt, row(mix_norm[l]), w_mix)

        a = _window_mixture(a_qkv, bsz, seq)

        lambda_init = 0.8 - 0.6 * math.exp(-0.3 * l)
        c = _diff_attention(c_qkv, bsz, seq, row(diff_lambda_q1[l]), row(diff_lambda_k1[l]),
                            row(diff_lambda_q2[l]), row(diff_lambda_k2[l]), row(diff_subln[l]),
                            lambda_init)

        xt = _mix_out(xt, a, u, c, pool_w[l].astype(BF16), row(pool_scale[l]),
                      w_out[l].astype(BF16), seq)

        xt = _ffn(xt, row(ffn2_norm[l]), ffn2_w_gate[l].astype(BF16), ffn2_w_up[l].astype(BF16),
                  ffn2_w_down[l].astype(BF16), row(final_norm), final_norm=(l == depth - 1))
    return xt.reshape(bsz, seq, d)
```

```python
import functools
import math

import jax
import jax.numpy as jnp
from jax import lax
from jax.experimental import pallas as pl
from jax.experimental.pallas import tpu as pltpu

F32 = jnp.float32
BF16 = jnp.bfloat16

EPS = 1e-6
NEG_INF = -1e30

HEAD_DIM = 64
LANES = 128
A_HEADS = 12
A_WIDTH = A_HEADS * HEAD_DIM
A_PATTERNS = ((128, 1), (512, 4), (2048, 16))
A_BLOCK = 128
A_SUPER = A_BLOCK * max(d for _, d in A_PATTERNS)
TASKS_PER_TRIP = 16
MIX_ROWS = 256
B_WINDOWS = (2, 4, 8, 16)
B_GROUP_DIM = 128
B_WIDTH = len(B_WINDOWS) * B_GROUP_DIM
B_HALO = 16
C_HEADS = 6
C_V_DIM = 2 * HEAD_DIM
C_WIDTH = C_HEADS * C_V_DIM
QK_SCALE = HEAD_DIM ** -0.5
LOG2_E = math.log2(math.e)
BF16_ROWS = 16

VMEM_LIMIT_BYTES = 56 * 1024 * 1024


def _params(n_axes):
    return pltpu.CompilerParams(
        dimension_semantics=("arbitrary",) * n_axes,
        vmem_limit_bytes=VMEM_LIMIT_BYTES,
    )


def _rmsnorm_rows(x, g):
    ms = jnp.mean(x * x, axis=-1, keepdims=True)
    return x * lax.rsqrt(ms + EPS) * g


def _ffn_kernel(x_ref, g_ref, wg_ref, wu_ref, wd_ref, fg_ref, o_ref, h_ref, *, final_norm):
    f = pl.program_id(1)

    @pl.when(f == 0)
    def _():
        x = x_ref[...]
        h_ref[...] = _rmsnorm_rows(x, g_ref[...]).astype(BF16)
        o_ref[...] = x

    h = h_ref[...]
    gate = jnp.dot(h, wg_ref[...], preferred_element_type=F32)
    up = jnp.dot(h, wu_ref[...], preferred_element_type=F32)
    act = (gate * (1.0 / (1.0 + jnp.exp(-gate))) * up).astype(BF16)
    o_ref[...] += 0.5 * jnp.dot(act, wd_ref[...], preferred_element_type=F32)

    if final_norm:
        @pl.when(f == pl.num_programs(1) - 1)
        def _():
            o_ref[...] = _rmsnorm_rows(o_ref[...], fg_ref[...])


def _ffn(x, g, wg, wu, wd, final_g, *, final_norm, tm=1024, tf=512):
    t, d = x.shape
    d_ff = wg.shape[1]
    assert t % tm == 0 and d_ff % tf == 0
    return pl.pallas_call(
        functools.partial(_ffn_kernel, final_norm=final_norm),
        grid=(t // tm, d_ff // tf),
        in_specs=[
            pl.BlockSpec((tm, d), lambda i, f: (i, 0), pipeline_mode=pl.Buffered(1)),
            pl.BlockSpec((1, d), lambda i, f: (0, 0)),
            pl.BlockSpec((d, tf), lambda i, f: (0, f)),
            pl.BlockSpec((d, tf), lambda i, f: (0, f)),
            pl.BlockSpec((tf, d), lambda i, f: (f, 0)),
            pl.BlockSpec((1, d), lambda i, f: (0, 0)),
        ],
        out_specs=pl.BlockSpec((tm, d), lambda i, f: (i, 0)),
        out_shape=jax.ShapeDtypeStruct((t, d), F32),
        scratch_shapes=[pltpu.VMEM((tm, d), BF16)],
        compiler_params=_params(2),
        name="ffn_final" if final_norm else "ffn",
    )(x, g, wg, wu, wd, final_g)


PROJ_TILE = A_WIDTH
PROJ_A_TILES = 3
PROJ_C_TILES = 3


def _mix_projection_kernel(x_ref, g_ref, w_ref, a_ref, c_ref, u_ref, h_ref):
    j = pl.program_id(1)

    @pl.when(j == 0)
    def _():
        h_ref[...] = _rmsnorm_rows(x_ref[...], g_ref[...]).astype(BF16)

    def product():
        return jnp.dot(h_ref[...], w_ref[...], preferred_element_type=F32)

    @pl.when(j < PROJ_A_TILES)
    def _():
        a_ref[...] = product()

    @pl.when((j >= PROJ_A_TILES) & (j < PROJ_A_TILES + PROJ_C_TILES))
    def _():
        c_ref[...] = product().astype(BF16)

    @pl.when(j == PROJ_A_TILES + PROJ_C_TILES)
    def _():
        u_ref[...] = product()


def _mix_projection(x, g, w, *, tm=1024):
    t, d = x.shape
    tm = min(tm, t)
    n_tiles = PROJ_A_TILES + PROJ_C_TILES + 1
    assert t % tm == 0 and w.shape[1] == n_tiles * PROJ_TILE
    last_a, first_c = PROJ_A_TILES - 1, PROJ_A_TILES
    return pl.pallas_call(
        _mix_projection_kernel,
        grid=(t // tm, n_tiles),
        in_specs=[
            pl.BlockSpec((tm, d), lambda i, j: (i, 0)),
            pl.BlockSpec((1, d), lambda i, j: (0, 0)),
            pl.BlockSpec((d, PROJ_TILE), lambda i, j: (0, j)),
        ],
        out_specs=[
            pl.BlockSpec((tm, PROJ_TILE), lambda i, j: (i, jnp.minimum(j, last_a))),
            pl.BlockSpec((tm, PROJ_TILE),
                         lambda i, j: (i, jnp.clip(j - first_c, 0, PROJ_C_TILES - 1))),
            pl.BlockSpec((tm, PROJ_TILE), lambda i, j: (i, 0)),
        ],
        out_shape=[
            jax.ShapeDtypeStruct((t, PROJ_A_TILES * PROJ_TILE), F32),
            jax.ShapeDtypeStruct((t, PROJ_C_TILES * PROJ_TILE), BF16),
            jax.ShapeDtypeStruct((t, PROJ_TILE), F32),
        ],
        scratch_shapes=[pltpu.VMEM((tm, d), BF16)],
        compiler_params=_params(2),
        name="mix_proj",
    )(x, g, w)


def _head_pair_attention(q2, k2, v2, valid, low_half):
    outs, lses = [], []
    for half in range(2):
        qm = jnp.where(low_half, q2, 0) if half == 0 else jnp.where(low_half, 0, q2)
        s = lax.dot_general(qm, k2, (((1,), (1,)), ((), ())), preferred_element_type=F32)
        s = jnp.where(valid, s, NEG_INF)
        m = jnp.max(s, axis=-1, keepdims=True)
        p = jnp.exp(s - m)
        l = jnp.sum(p, axis=-1, keepdims=True)
        o = jnp.dot(p.astype(BF16), v2, preferred_element_type=F32)
        outs.append(o / l)
        lses.append(jnp.broadcast_to(m + jnp.log(l), (A_BLOCK, LANES)))
    return jnp.where(low_half, outs[0], outs[1]), jnp.where(low_half, lses[0], lses[1])


def _window_mixture_kernel(q_ref, k_ref, v_ref, a_ref, kk_ref, vv_ref, o_ref, lse_ref):
    first = pl.program_id(2) == 0

    @pl.when(first)
    def _():
        kk_ref[0:A_SUPER, :] = jnp.zeros((A_SUPER, LANES), F32)
        vv_ref[0:A_SUPER, :] = jnp.zeros((A_SUPER, LANES), F32)

    kk_ref[A_SUPER:, :] = k_ref[0]
    vv_ref[A_SUPER:, :] = v_ref[0]

    qi = lax.broadcasted_iota(jnp.int32, (A_BLOCK, 2 * A_BLOCK), 0)
    kj = lax.broadcasted_iota(jnp.int32, (A_BLOCK, 2 * A_BLOCK), 1)
    band = kj <= qi + A_BLOCK
    low_half = lax.broadcasted_iota(jnp.int32, (A_BLOCK, LANES), 1) < HEAD_DIM
    n_tasks = A_SUPER // A_BLOCK

    for p, (_, dilation) in enumerate(A_PATTERNS):
        def tasks(g, carry, p=p, dilation=dilation):
            for u in range(TASKS_PER_TRIP):
                t = g * TASKS_PER_TRIP + u
                sub, r = t // dilation, t % dilation
                start = sub * (A_BLOCK * dilation) + r
                lowest = jnp.where(first & (sub == 0), A_BLOCK, qi)
                valid = band & (kj >= lowest)
                q_rows = pl.ds(start, A_BLOCK, stride=dilation)
                kv_rows = pl.ds(A_SUPER + start - A_BLOCK * dilation, 2 * A_BLOCK, stride=dilation)
                q2 = (q_ref[0, q_rows, :] * QK_SCALE).astype(BF16)
                k2 = kk_ref[kv_rows, :].astype(BF16)
                v2 = vv_ref[kv_rows, :].astype(BF16)
                o_ref[p, q_rows, :], lse_ref[p, q_rows, :] = _head_pair_attention(
                    q2, k2, v2, valid, low_half)
            return carry

        lax.fori_loop(0, n_tasks // TASKS_PER_TRIP, tasks, 0)

    kk_ref[0:A_SUPER, :] = k_ref[0]
    vv_ref[0:A_SUPER, :] = v_ref[0]

    def mixture(c, carry):
        rows = pl.ds(pl.multiple_of(c * MIX_ROWS, MIX_ROWS), MIX_ROWS)
        lses = [lse_ref[p, rows, :] for p in range(len(A_PATTERNS))]
        mx = functools.reduce(jnp.maximum, lses)
        es = [jnp.exp(l - mx) for l in lses]
        den = functools.reduce(lambda x, y: x + y, es)
        a = functools.reduce(lambda x, y: x + y,
                             [(e / den) * o_ref[p, rows, :] for p, e in enumerate(es)])
        a_ref[0, rows, :] = a.astype(a_ref.dtype)
        return carry

    lax.fori_loop(0, A_SUPER // MIX_ROWS, mixture, 0)


def _window_mixture(a_qkv, bsz, seq):
    assert seq % A_SUPER == 0 and all(w // d == A_BLOCK for w, d in A_PATTERNS)
    view = a_qkv.reshape(bsz, seq, 3 * A_WIDTH)
    per_part = A_WIDTH // LANES
    part = lambda p: pl.BlockSpec((1, A_SUPER, LANES), lambda b, hp, i: (b, i, p * per_part + hp))
    n_pat = len(A_PATTERNS)
    out = pl.pallas_call(
        _window_mixture_kernel,
        grid=(bsz, per_part, seq // A_SUPER),
        in_specs=[part(0), part(1), part(2)],
        out_specs=part(0),
        out_shape=jax.ShapeDtypeStruct((bsz, seq, A_WIDTH), BF16),
        scratch_shapes=[
            pltpu.VMEM((2 * A_SUPER, LANES), F32),
            pltpu.VMEM((2 * A_SUPER, LANES), F32),
            pltpu.VMEM((n_pat, A_SUPER, LANES), F32),
            pltpu.VMEM((n_pat, A_SUPER, LANES), F32),
        ],
        compiler_params=_params(3),
        name="window_mixture",
    )(view, view, view)
    return out.reshape(bsz * seq, A_WIDTH)


def _diff_attn_kernel(q_ref, k_ref, v_ref, lq1_ref, lk1_ref, lq2_ref, lk2_ref, g_ref, o_ref,
                      qt_ref, vt_ref, s_ref, m_ref, acc_ref, *, tq, tk, lambda_init):
    n_chunks = tq // tk
    qb = pl.program_id(2)
    seq = v_ref.shape[1]

    @pl.when(qb == 0)
    def _():
        def transpose_rows(c, carry):
            off = pl.multiple_of(c * tq, tq)
            vt_ref[0:C_V_DIM, pl.ds(off, tq)] = (
                v_ref[0, pl.ds(off, tq), :].astype(F32).T.astype(BF16))
            return carry
        lax.fori_loop(0, seq // tq, transpose_rows, 0)
        vt_ref[C_V_DIM:, :] = jnp.ones((vt_ref.shape[0] - C_V_DIM, seq), BF16)

    qt = (q_ref[0].astype(F32) * (QK_SCALE * LOG2_E)).T.astype(BF16)
    dim = lax.broadcasted_iota(jnp.int32, (2 * HEAD_DIM, tq), 0)
    zeros = jnp.zeros_like(qt)
    qt_ref[0] = jnp.where(dim < HEAD_DIM, qt, zeros)
    qt_ref[1] = jnp.where(dim < HEAD_DIM, zeros, qt)
    m_ref[...] = jnp.full(m_ref.shape, NEG_INF, F32)
    acc_ref[...] = jnp.zeros(acc_ref.shape, F32)

    def score(slot, key_off, diag_chunk):
        k = k_ref[0, pl.ds(pl.multiple_of(key_off, tk), tk), :]
        for mi in range(2):
            st = jnp.dot(k, qt_ref[mi], preferred_element_type=F32)
            if diag_chunk is not None:
                key = lax.broadcasted_iota(jnp.int32, (tk, tq), 0) + diag_chunk * tk
                qry = lax.broadcasted_iota(jnp.int32, (tk, tq), 1)
                st = jnp.where(key <= qry, st, NEG_INF)
            s_ref[slot, mi] = st

    def absorb(slot, key_off):
        vt = vt_ref[:, pl.ds(pl.multiple_of(key_off, tk), tk)]
        for mi in range(2):
            m_old = m_ref[mi]
            m_new = jnp.maximum(m_old, jnp.max(s_ref[slot, mi], axis=0, keepdims=True))
            alpha = jnp.exp2(m_old - m_new)
            pt = jnp.exp2(s_ref[slot, mi] - m_new)
            m_ref[mi] = m_new
            pv = jnp.dot(vt, pt.astype(BF16), preferred_element_type=F32)
            acc_ref[mi] = alpha * acc_ref[mi] + pv

    diag = qb * tq
    last = n_chunks - 1
    score(0, diag, 0)
    for c in range(1, n_chunks):
        score(c % 2, diag + c * tk, c)
        absorb((c - 1) % 2, diag + (c - 1) * tk)

    def block(i, carry):
        off = i * tq
        score(0, off, None)
        absorb(1, jnp.where(i == 0, diag, off - tq) + last * tk)
        for c in range(1, n_chunks):
            score(c % 2, off + c * tk, None)
            absorb((c - 1) % 2, off + (c - 1) * tk)
        return carry

    lax.fori_loop(0, qb, block, 0)
    absorb(1, jnp.where(qb == 0, diag, diag - tq) + last * tk)

    lam = (jnp.exp(jnp.sum(lq1_ref[...] * lk1_ref[...], axis=-1, keepdims=True))
           - jnp.exp(jnp.sum(lq2_ref[...] * lk2_ref[...], axis=-1, keepdims=True))
           + lambda_init)
    l1 = acc_ref[0, C_V_DIM:C_V_DIM + 1, :]
    l2 = acc_ref[1, C_V_DIM:C_V_DIM + 1, :]
    ot = acc_ref[0, 0:C_V_DIM, :] / l1 - lam * (acc_ref[1, 0:C_V_DIM, :] / l2)
    ms = jnp.mean(ot * ot, axis=0, keepdims=True)
    ot = ot * lax.rsqrt(ms + EPS) * g_ref[...] * (1.0 - lambda_init)
    o_ref[0] = ot.T.astype(o_ref.dtype)


def _diff_attention(proj, bsz, seq, lq1, lk1, lq2, lk2, subln_g, lambda_init, *, tq=1024, tk=512):
    tq = min(tq, seq)
    assert seq % tq == 0 and tq % (2 * tk) == 0
    cols = proj.shape[-1]
    view = proj.reshape(bsz, seq, cols)
    q0 = 0
    k0 = q0 + C_HEADS
    v0 = k0 + C_HEADS
    vec = lambda: pl.BlockSpec((1, HEAD_DIM), lambda b, h, i: (0, 0))
    out = pl.pallas_call(
        functools.partial(_diff_attn_kernel, tq=tq, tk=tk, lambda_init=lambda_init),
        grid=(bsz, C_HEADS, seq // tq),
        in_specs=[
            pl.BlockSpec((1, tq, LANES), lambda b, h, i: (b, i, q0 + h)),
            pl.BlockSpec((1, seq, LANES), lambda b, h, i: (b, 0, k0 + h)),
            pl.BlockSpec((1, seq, LANES), lambda b, h, i: (b, 0, v0 + h)),
            vec(), vec(), vec(), vec(),
            pl.BlockSpec((C_V_DIM, 1), lambda b, h, i: (0, 0)),
        ],
        out_specs=pl.BlockSpec((1, tq, LANES), lambda b, h, i: (b, i, h)),
        out_shape=jax.ShapeDtypeStruct((bsz, seq, C_WIDTH), BF16),
        scratch_shapes=[
            pltpu.VMEM((2, C_V_DIM, tq), BF16),
            pltpu.VMEM((C_V_DIM + BF16_ROWS, seq), BF16),
            pltpu.VMEM((2, 2, tk, tq), F32),
            pltpu.VMEM((2, 1, tq), F32),
            pltpu.VMEM((2, C_V_DIM + BF16_ROWS, tq), F32),
        ],
        compiler_params=_params(3),
        name="diff_attn",
    )(view, view, view, lq1, lk1, lq2, lk2, subln_g.reshape(C_V_DIM, 1))
    return out.reshape(bsz * seq, C_WIDTH)


def _mix_out_kernel(x_ref, a_ref, u_ref, halo_ref, c_ref, pw_ref, ps_ref, wo_ref, out_ref,
                    *, tm, rows_per_seq):
    i = pl.program_id(0)

    acc = jnp.dot(a_ref[...], wo_ref[0:A_WIDTH, :], preferred_element_type=F32)

    t_in_seq = (i * tm) % rows_per_seq + lax.broadcasted_iota(jnp.int32, (tm, 1), 0)
    first = (i * tm) % rows_per_seq == 0
    halo = jnp.where(first, jnp.zeros((B_HALO, B_WIDTH), F32), halo_ref[...])
    u = u_ref[...]
    ext = jnp.concatenate([halo, u], axis=0)
    ys = []
    win = ext
    width = 1
    for g, w in enumerate(B_WINDOWS):
        while width < w:
            win = win + pltpu.roll(win, width, 0)
            width *= 2
        gs = slice(g * B_GROUP_DIM, (g + 1) * B_GROUP_DIM)
        cnt = jnp.minimum(t_in_seq + 1, w).astype(F32)
        diff = win[B_HALO:, gs] / cnt - u[:, gs]
        ys.append(jnp.dot(diff.astype(BF16), pw_ref[g], preferred_element_type=F32))
    b = jnp.concatenate(ys, axis=-1) * ps_ref[...]
    acc += jnp.dot(b.astype(BF16), wo_ref[A_WIDTH:A_WIDTH + B_WIDTH, :], preferred_element_type=F32)

    acc += jnp.dot(c_ref[...], wo_ref[A_WIDTH + B_WIDTH:, :], preferred_element_type=F32)
    out_ref[...] = x_ref[...] + acc


def _mix_out(x, a, u, c, pool_w, pool_scale, w_out, seq, *, tm=256):
    t, d = x.shape
    assert t % tm == 0 and seq % tm == 0 and tm % B_HALO == 0
    row = lambda width: pl.BlockSpec((tm, width), lambda i: (i, 0))
    halo_blocks = tm // B_HALO
    return pl.pallas_call(
        functools.partial(_mix_out_kernel, tm=tm, rows_per_seq=seq),
        grid=(t // tm,),
        in_specs=[
            row(d),
            row(A_WIDTH),
            row(B_WIDTH),
            pl.BlockSpec((B_HALO, B_WIDTH), lambda i: (jnp.maximum(i * halo_blocks - 1, 0), 0)),
            row(C_WIDTH),
            pl.BlockSpec(pool_w.shape, lambda i: (0, 0, 0)),
            pl.BlockSpec((1, B_WIDTH), lambda i: (0, 0)),
            pl.BlockSpec(w_out.shape, lambda i: (0, 0)),
        ],
        out_specs=row(d),
        out_shape=jax.ShapeDtypeStruct((t, d), F32),
        compiler_params=_params(1),
        name="mix_out",
    )(x, a, u, u, c, pool_w, pool_scale, w_out)


def kernel(x, ffn1_norm, ffn1_w_gate, ffn1_w_up, ffn1_w_down, mix_norm, w_in, pool_w, pool_scale,
           diff_lambda_q1, diff_lambda_k1, diff_lambda_q2, diff_lambda_k2, diff_subln, w_out,
           ffn2_norm, ffn2_w_gate, ffn2_w_up, ffn2_w_down, final_norm):
    bsz, seq, d = x.shape
    depth = w_in.shape[0]
    b_lo, c_lo = 3 * A_WIDTH, 3 * A_WIDTH + B_WIDTH
    assert w_in.shape[2] == c_lo + 3 * C_WIDTH and d == A_WIDTH + B_WIDTH + C_WIDTH
    xt = x.reshape(bsz * seq, d)
    row = lambda v: v.reshape(1, -1).astype(F32)
    for l in range(depth):
        xt = _ffn(xt, row(ffn1_norm[l]), ffn1_w_gate[l].astype(BF16), ffn1_w_up[l].astype(BF16),
                  ffn1_w_down[l].astype(BF16), row(final_norm), final_norm=False)

        w_l = w_in[l].astype(BF16)
        w_mix = jnp.concatenate([w_l[:, :b_lo], w_l[:, c_lo:], w_l[:, b_lo:c_lo],
                                 jnp.zeros((d, PROJ_TILE - B_WIDTH), BF16)], axis=1)
        a_qkv, c_qkv, u = _mix_projection(xt, row(mix_norm[l]), w_mix)

        a = _window_mixture(a_qkv, bsz, seq)

        lambda_init = 0.8 - 0.6 * math.exp(-0.3 * l)
        c = _diff_attention(c_qkv, bsz, seq, row(diff_lambda_q1[l]), row(diff_lambda_k1[l]),
                            row(diff_lambda_q2[l]), row(diff_lambda_k2[l]), row(diff_subln[l]),
                            lambda_init)

        xt = _mix_out(xt, a, u, c, pool_w[l].astype(BF16), row(pool_scale[l]),
                      w_out[l].astype(BF16), seq)

        xt = _ffn(xt, row(ffn2_norm[l]), ffn2_w_gate[l].astype(BF16), ffn2_w_up[l].astype(BF16),
                  ffn2_w_down[l].astype(BF16), row(final_norm), final_norm=(l == depth - 1))
    return xt.reshape(bsz, seq, d)
```

```python
import functools
import math

import jax
import jax.numpy as jnp
from jax import lax
from jax.experimental import pallas as pl
from jax.experimental.pallas import tpu as pltpu

F32 = jnp.float32
BF16 = jnp.bfloat16

EPS = 1e-6
NEG_INF = -1e30

HEAD_DIM = 64
LANES = 128
A_HEADS = 12
A_WIDTH = A_HEADS * HEAD_DIM
A_PATTERNS = ((128, 1), (512, 4), (2048, 16))
A_BLOCK = 128
A_SUPER = A_BLOCK * max(d for _, d in A_PATTERNS)
TASKS_PER_TRIP = 16
MIX_ROWS = 256
B_WINDOWS = (2, 4, 8, 16)
B_GROUP_DIM = 128
B_WIDTH = len(B_WINDOWS) * B_GROUP_DIM
B_HALO = 16
C_HEADS = 6
C_V_DIM = 2 * HEAD_DIM
C_WIDTH = C_HEADS * C_V_DIM
QK_SCALE = HEAD_DIM ** -0.5
LOG2_E = math.log2(math.e)
BF16_ROWS = 16

VMEM_LIMIT_BYTES = 56 * 1024 * 1024


def _params(n_axes):
    return pltpu.CompilerParams(
        dimension_semantics=("arbitrary",) * n_axes,
        vmem_limit_bytes=VMEM_LIMIT_BYTES,
    )


def _rmsnorm_rows(x, g):
    ms = jnp.mean(x * x, axis=-1, keepdims=True)
    return x * lax.rsqrt(ms + EPS) * g


def _ffn_kernel(x_ref, g_ref, wg_ref, wu_ref, wd_ref, fg_ref, o_ref, h_ref, *, final_norm):
    f = pl.program_id(1)

    @pl.when(f == 0)
    def _():
        x = x_ref[...]
        h_ref[...] = _rmsnorm_rows(x, g_ref[...]).astype(BF16)
        o_ref[...] = x

    h = h_ref[...]
    gate = jnp.dot(h, wg_ref[...], preferred_element_type=F32)
    up = jnp.dot(h, wu_ref[...], preferred_element_type=F32)
    act = (gate * (1.0 / (1.0 + jnp.exp(-gate))) * up).astype(BF16)
    o_ref[...] += 0.5 * jnp.dot(act, wd_ref[...], preferred_element_type=F32)

    if final_norm:
        @pl.when(f == pl.num_programs(1) - 1)
        def _():
            o_ref[...] = _rmsnorm_rows(o_ref[...], fg_ref[...])


def _ffn(x, g, wg, wu, wd, final_g, *, final_norm, tm=1024, tf=512):
    t, d = x.shape
    d_ff = wg.shape[1]
    assert t % tm == 0 and d_ff % tf == 0
    return pl.pallas_call(
        functools.partial(_ffn_kernel, final_norm=final_norm),
        grid=(t // tm, d_ff // tf),
        in_specs=[
            pl.BlockSpec((tm, d), lambda i, f: (i, 0)),
            pl.BlockSpec((1, d), lambda i, f: (0, 0)),
            pl.BlockSpec((d, tf), lambda i, f: (0, f)),
            pl.BlockSpec((d, tf), lambda i, f: (0, f)),
            pl.BlockSpec((tf, d), lambda i, f: (f, 0)),
            pl.BlockSpec((1, d), lambda i, f: (0, 0)),
        ],
        out_specs=pl.BlockSpec((tm, d), lambda i, f: (i, 0)),
        out_shape=jax.ShapeDtypeStruct((t, d), F32),
        scratch_shapes=[pltpu.VMEM((tm, d), BF16)],
        compiler_params=_params(2),
        name="ffn_final" if final_norm else "ffn",
    )(x, g, wg, wu, wd, final_g)


PROJ_TILE = A_WIDTH
PROJ_A_TILES = 3
PROJ_C_TILES = 3


def _mix_projection_kernel(x_ref, g_ref, w_ref, a_ref, c_ref, u_ref, h_ref):
    j = pl.program_id(1)

    @pl.when(j == 0)
    def _():
        h_ref[...] = _rmsnorm_rows(x_ref[...], g_ref[...]).astype(BF16)

    def product():
        return jnp.dot(h_ref[...], w_ref[...], preferred_element_type=F32)

    @pl.when(j < PROJ_A_TILES)
    def _():
        a_ref[...] = product()

    @pl.when((j >= PROJ_A_TILES) & (j < PROJ_A_TILES + PROJ_C_TILES))
    def _():
        c_ref[...] = product().astype(BF16)

    @pl.when(j == PROJ_A_TILES + PROJ_C_TILES)
    def _():
        u_ref[...] = product()


def _mix_projection(x, g, w, *, tm=1024):
    t, d = x.shape
    tm = min(tm, t)
    n_tiles = PROJ_A_TILES + PROJ_C_TILES + 1
    assert t % tm == 0 and w.shape[1] == n_tiles * PROJ_TILE
    last_a, first_c = PROJ_A_TILES - 1, PROJ_A_TILES
    return pl.pallas_call(
        _mix_projection_kernel,
        grid=(t // tm, n_tiles),
        in_specs=[
            pl.BlockSpec((tm, d), lambda i, j: (i, 0)),
            pl.BlockSpec((1, d), lambda i, j: (0, 0)),
            pl.BlockSpec((d, PROJ_TILE), lambda i, j: (0, j)),
        ],
        out_specs=[
            pl.BlockSpec((tm, PROJ_TILE), lambda i, j: (i, jnp.minimum(j, last_a))),
            pl.BlockSpec((tm, PROJ_TILE),
                         lambda i, j: (i, jnp.clip(j - first_c, 0, PROJ_C_TILES - 1))),
            pl.BlockSpec((tm, PROJ_TILE), lambda i, j: (i, 0)),
        ],
        out_shape=[
            jax.ShapeDtypeStruct((t, PROJ_A_TILES * PROJ_TILE), F32),
            jax.ShapeDtypeStruct((t, PROJ_C_TILES * PROJ_TILE), BF16),
            jax.ShapeDtypeStruct((t, PROJ_TILE), F32),
        ],
        scratch_shapes=[pltpu.VMEM((tm, d), BF16)],
        compiler_params=_params(2),
        name="mix_proj",
    )(x, g, w)


def _head_pair_attention(q2, k2, v2, valid, low_half):
    outs, lses = [], []
    for half in range(2):
        qm = jnp.where(low_half, q2, 0) if half == 0 else jnp.where(low_half, 0, q2)
        s = lax.dot_general(qm, k2, (((1,), (1,)), ((), ())), preferred_element_type=F32)
        s = jnp.where(valid, s, NEG_INF)
        m = jnp.max(s, axis=-1, keepdims=True)
        p = jnp.exp(s - m)
        l = jnp.sum(p, axis=-1, keepdims=True)
        o = jnp.dot(p.astype(BF16), v2, preferred_element_type=F32)
        outs.append(o / l)
        lses.append(jnp.broadcast_to(m + jnp.log(l), (A_BLOCK, LANES)))
    return jnp.where(low_half, outs[0], outs[1]), jnp.where(low_half, lses[0], lses[1])


def _window_mixture_kernel(q_ref, k_ref, v_ref, a_ref, kk_ref, vv_ref, o_ref, lse_ref):
    first = pl.program_id(2) == 0

    @pl.when(first)
    def _():
        kk_ref[0:A_SUPER, :] = jnp.zeros((A_SUPER, LANES), F32)
        vv_ref[0:A_SUPER, :] = jnp.zeros((A_SUPER, LANES), F32)

    kk_ref[A_SUPER:, :] = k_ref[0]
    vv_ref[A_SUPER:, :] = v_ref[0]

    qi = lax.broadcasted_iota(jnp.int32, (A_BLOCK, 2 * A_BLOCK), 0)
    kj = lax.broadcasted_iota(jnp.int32, (A_BLOCK, 2 * A_BLOCK), 1)
    band = kj <= qi + A_BLOCK
    low_half = lax.broadcasted_iota(jnp.int32, (A_BLOCK, LANES), 1) < HEAD_DIM
    n_tasks = A_SUPER // A_BLOCK

    for p, (_, dilation) in enumerate(A_PATTERNS):
        def tasks(g, carry, p=p, dilation=dilation):
            for u in range(TASKS_PER_TRIP):
                t = g * TASKS_PER_TRIP + u
                sub, r = t // dilation, t % dilation
                start = sub * (A_BLOCK * dilation) + r
                lowest = jnp.where(first & (sub == 0), A_BLOCK, qi)
                valid = band & (kj >= lowest)
                q_rows = pl.ds(start, A_BLOCK, stride=dilation)
                kv_rows = pl.ds(A_SUPER + start - A_BLOCK * dilation, 2 * A_BLOCK, stride=dilation)
                q2 = (q_ref[0, q_rows, :] * QK_SCALE).astype(BF16)
                k2 = kk_ref[kv_rows, :].astype(BF16)
                v2 = vv_ref[kv_rows, :].astype(BF16)
                o_ref[p, q_rows, :], lse_ref[p, q_rows, :] = _head_pair_attention(
                    q2, k2, v2, valid, low_half)
            return carry

        lax.fori_loop(0, n_tasks // TASKS_PER_TRIP, tasks, 0)

    kk_ref[0:A_SUPER, :] = k_ref[0]
    vv_ref[0:A_SUPER, :] = v_ref[0]

    def mixture(c, carry):
        rows = pl.ds(pl.multiple_of(c * MIX_ROWS, MIX_ROWS), MIX_ROWS)
        lses = [lse_ref[p, rows, :] for p in range(len(A_PATTERNS))]
        mx = functools.reduce(jnp.maximum, lses)
        es = [jnp.exp(l - mx) for l in lses]
        den = functools.reduce(lambda x, y: x + y, es)
        a = functools.reduce(lambda x, y: x + y,
                             [(e / den) * o_ref[p, rows, :] for p, e in enumerate(es)])
        a_ref[0, rows, :] = a.astype(a_ref.dtype)
        return carry

    lax.fori_loop(0, A_SUPER // MIX_ROWS, mixture, 0)


def _window_mixture(a_qkv, bsz, seq):
    assert seq % A_SUPER == 0 and all(w // d == A_BLOCK for w, d in A_PATTERNS)
    view = a_qkv.reshape(bsz, seq, 3 * A_WIDTH)
    per_part = A_WIDTH // LANES
    part = lambda p: pl.BlockSpec((1, A_SUPER, LANES), lambda b, hp, i: (b, i, p * per_part + hp))
    n_pat = len(A_PATTERNS)
    out = pl.pallas_call(
        _window_mixture_kernel,
        grid=(bsz, per_part, seq // A_SUPER),
        in_specs=[part(0), part(1), part(2)],
        out_specs=part(0),
        out_shape=jax.ShapeDtypeStruct((bsz, seq, A_WIDTH), BF16),
        scratch_shapes=[
            pltpu.VMEM((2 * A_SUPER, LANES), F32),
            pltpu.VMEM((2 * A_SUPER, LANES), F32),
            pltpu.VMEM((n_pat, A_SUPER, LANES), F32),
            pltpu.VMEM((n_pat, A_SUPER, LANES), F32),
        ],
        compiler_params=_params(3),
        name="window_mixture",
    )(view, view, view)
    return out.reshape(bsz * seq, A_WIDTH)


def _diff_attn_kernel(q_ref, k_ref, v_ref, lq1_ref, lk1_ref, lq2_ref, lk2_ref, g_ref, o_ref,
                      qt_ref, vt_ref, s_ref, cmax_ref, m_ref, acc_ref, *, tq, tk, lambda_init):
    n_chunks = tq // tk
    qb = pl.program_id(2)
    seq = v_ref.shape[1]

    @pl.when(qb == 0)
    def _():
        def transpose_rows(c, carry):
            off = pl.multiple_of(c * tq, tq)
            vt_ref[0:C_V_DIM, pl.ds(off, tq)] = (
                v_ref[0, pl.ds(off, tq), :].astype(F32).T.astype(BF16))
            return carry
        lax.fori_loop(0, seq // tq, transpose_rows, 0)
        vt_ref[C_V_DIM:, :] = jnp.ones((vt_ref.shape[0] - C_V_DIM, seq), BF16)

    qt = (q_ref[0].astype(F32) * (QK_SCALE * LOG2_E)).T.astype(BF16)
    dim = lax.broadcasted_iota(jnp.int32, (2 * HEAD_DIM, tq), 0)
    zeros = jnp.zeros_like(qt)
    qt_ref[0] = jnp.where(dim < HEAD_DIM, qt, zeros)
    qt_ref[1] = jnp.where(dim < HEAD_DIM, zeros, qt)
    m_ref[...] = jnp.full(m_ref.shape, NEG_INF, F32)
    acc_ref[...] = jnp.zeros(acc_ref.shape, F32)

    def score(slot, key_off, diag_chunk):
        k = k_ref[0, pl.ds(pl.multiple_of(key_off, tk), tk), :]
        for mi in range(2):
            st = jnp.dot(k, qt_ref[mi], preferred_element_type=F32)
            if diag_chunk is not None:
                key = lax.broadcasted_iota(jnp.int32, (tk, tq), 0) + diag_chunk * tk
                qry = lax.broadcasted_iota(jnp.int32, (tk, tq), 1)
                st = jnp.where(key <= qry, st, NEG_INF)
            s_ref[slot, mi] = st
            cmax_ref[slot, mi] = jnp.max(st, axis=0, keepdims=True)

    def absorb(slot, key_off):
        vt = vt_ref[:, pl.ds(pl.multiple_of(key_off, tk), tk)]
        for mi in range(2):
            m_old = m_ref[mi]
            m_new = jnp.maximum(m_old, cmax_ref[slot, mi])
            alpha = jnp.exp2(m_old - m_new)
            pt = jnp.exp2(s_ref[slot, mi] - m_new)
            m_ref[mi] = m_new
            pv = jnp.dot(vt, pt.astype(BF16), preferred_element_type=F32)
            acc_ref[mi] = alpha * acc_ref[mi] + pv

    diag = qb * tq
    last = n_chunks - 1
    score(0, diag, 0)
    for c in range(1, n_chunks):
        score(c % 2, diag + c * tk, c)
        absorb((c - 1) % 2, diag + (c - 1) * tk)

    def block(i, carry):
        off = i * tq
        score(0, off, None)
        absorb(1, jnp.where(i == 0, diag, off - tq) + last * tk)
        for c in range(1, n_chunks):
            score(c % 2, off + c * tk, None)
            absorb((c - 1) % 2, off + (c - 1) * tk)
        return carry

    lax.fori_loop(0, qb, block, 0)
    absorb(1, jnp.where(qb == 0, diag, diag - tq) + last * tk)

    lam = (jnp.exp(jnp.sum(lq1_ref[...] * lk1_ref[...], axis=-1, keepdims=True))
           - jnp.exp(jnp.sum(lq2_ref[...] * lk2_ref[...], axis=-1, keepdims=True))
           + lambda_init)
    l1 = acc_ref[0, C_V_DIM:C_V_DIM + 1, :]
    l2 = acc_ref[1, C_V_DIM:C_V_DIM + 1, :]
    ot = acc_ref[0, 0:C_V_DIM, :] / l1 - lam * (acc_ref[1, 0:C_V_DIM, :] / l2)
    ms = jnp.mean(ot * ot, axis=0, keepdims=True)
    ot = ot * lax.rsqrt(ms + EPS) * g_ref[...] * (1.0 - lambda_init)
    o_ref[0] = ot.T.astype(o_ref.dtype)


def _diff_attention(proj, bsz, seq, lq1, lk1, lq2, lk2, subln_g, lambda_init, *, tq=1024, tk=512):
    tq = min(tq, seq)
    assert seq % tq == 0 and tq % (2 * tk) == 0
    cols = proj.shape[-1]
    view = proj.reshape(bsz, seq, cols)
    q0 = 0
    k0 = q0 + C_HEADS
    v0 = k0 + C_HEADS
    vec = lambda: pl.BlockSpec((1, HEAD_DIM), lambda b, h, i: (0, 0))
    out = pl.pallas_call(
        functools.partial(_diff_attn_kernel, tq=tq, tk=tk, lambda_init=lambda_init),
        grid=(bsz, C_HEADS, seq // tq),
        in_specs=[
            pl.BlockSpec((1, tq, LANES), lambda b, h, i: (b, i, q0 + h)),
            pl.BlockSpec((1, seq, LANES), lambda b, h, i: (b, 0, k0 + h)),
            pl.BlockSpec((1, seq, LANES), lambda b, h, i: (b, 0, v0 + h)),
            vec(), vec(), vec(), vec(),
            pl.BlockSpec((C_V_DIM, 1), lambda b, h, i: (0, 0)),
        ],
        out_specs=pl.BlockSpec((1, tq, LANES), lambda b, h, i: (b, i, h)),
        out_shape=jax.ShapeDtypeStruct((bsz, seq, C_WIDTH), BF16),
        scratch_shapes=[
            pltpu.VMEM((2, C_V_DIM, tq), BF16),
            pltpu.VMEM((C_V_DIM + BF16_ROWS, seq), BF16),
            pltpu.VMEM((2, 2, tk, tq), F32),
            pltpu.VMEM((2, 2, 1, tq), F32),
            pltpu.VMEM((2, 1, tq), F32),
            pltpu.VMEM((2, C_V_DIM + BF16_ROWS, tq), F32),
        ],
        compiler_params=_params(3),
        name="diff_attn",
    )(view, view, view, lq1, lk1, lq2, lk2, subln_g.reshape(C_V_DIM, 1))
    return out.reshape(bsz * seq, C_WIDTH)


def _mix_out_kernel(x_ref, a_ref, u_ref, halo_ref, c_ref, pw_ref, ps_ref, wo_ref, out_ref,
                    *, tm, rows_per_seq):
    i = pl.program_id(0)

    acc = jnp.dot(a_ref[...], wo_ref[0:A_WIDTH, :], preferred_element_type=F32)

    t_in_seq = (i * tm) % rows_per_seq + lax.broadcasted_iota(jnp.int32, (tm, 1), 0)
    first = (i * tm) % rows_per_seq == 0
    halo = jnp.where(first, jnp.zeros((B_HALO, B_WIDTH), F32), halo_ref[...])
    u = u_ref[...]
    ext = jnp.concatenate([halo, u], axis=0)
    ys = []
    win = ext
    width = 1
    for g, w in enumerate(B_WINDOWS):
        while width < w:
            win = win + pltpu.roll(win, width, 0)
            width *= 2
        gs = slice(g * B_GROUP_DIM, (g + 1) * B_GROUP_DIM)
        cnt = jnp.minimum(t_in_seq + 1, w).astype(F32)
        diff = win[B_HALO:, gs] / cnt - u[:, gs]
        ys.append(jnp.dot(diff.astype(BF16), pw_ref[g], preferred_element_type=F32))
    b = jnp.concatenate(ys, axis=-1) * ps_ref[...]
    acc += jnp.dot(b.astype(BF16), wo_ref[A_WIDTH:A_WIDTH + B_WIDTH, :], preferred_element_type=F32)

    acc += jnp.dot(c_ref[...], wo_ref[A_WIDTH + B_WIDTH:, :], preferred_element_type=F32)
    out_ref[...] = x_ref[...] + acc


def _mix_out(x, a, u, c, pool_w, pool_scale, w_out, seq, *, tm=256):
    t, d = x.shape
    assert t % tm == 0 and seq % tm == 0 and tm % B_HALO == 0
    row = lambda width: pl.BlockSpec((tm, width), lambda i: (i, 0))
    halo_blocks = tm // B_HALO
    return pl.pallas_call(
        functools.partial(_mix_out_kernel, tm=tm, rows_per_seq=seq),
        grid=(t // tm,),
        in_specs=[
            row(d),
            row(A_WIDTH),
            row(B_WIDTH),
            pl.BlockSpec((B_HALO, B_WIDTH), lambda i: (jnp.maximum(i * halo_blocks - 1, 0), 0)),
            row(C_WIDTH),
            pl.BlockSpec(pool_w.shape, lambda i: (0, 0, 0)),
            pl.BlockSpec((1, B_WIDTH), lambda i: (0, 0)),
            pl.BlockSpec(w_out.shape, lambda i: (0, 0)),
        ],
        out_specs=row(d),
        out_shape=jax.ShapeDtypeStruct((t, d), F32),
        compiler_params=_params(1),
        name="mix_out",
    )(x, a, u, u, c, pool_w, pool_scale, w_out)


def kernel(x, ffn1_norm, ffn1_w_gate, ffn1_w_up, ffn1_w_down, mix_norm, w_in, pool_w, pool_scale,
           diff_lambda_q1, diff_lambda_k1, diff_lambda_q2, diff_lambda_k2, diff_subln, w_out,
           ffn2_norm, ffn2_w_gate, ffn2_w_up, ffn2_w_down, final_norm):
    bsz, seq, d = x.shape
    depth = w_in.shape[0]
    b_lo, c_lo = 3 * A_WIDTH, 3 * A_WIDTH + B_WIDTH
    assert w_in.shape[2] == c_lo + 3 * C_WIDTH and d == A_WIDTH + B_WIDTH + C_WIDTH
    xt = x.reshape(bsz * seq, d)
    row = lambda v: v.reshape(1, -1).astype(F32)
    for l in range(depth):
        xt = _ffn(xt, row(ffn1_norm[l]), ffn1_w_gate[l].astype(BF16), ffn1_w_up[l].astype(BF16),
                  ffn1_w_down[l].astype(BF16), row(final_norm), final_norm=False)

        w_l = w_in[l].astype(BF16)
        w_mix = jnp.concatenate([w_l[:, :b_lo], w_l[:, c_lo:], w_l[:, b_lo:c_lo],
                                 jnp.zeros((d, PROJ_TILE - B_WIDTH), BF16)], axis=1)
        a_qkv, c_qkv, u = _mix_projection(xt, row(mix_norm[l]), w_mix)

        a = _window_mixture(a_qkv, bsz, seq)

        lambda_init = 0.8 - 0.6 * math.exp(-0.3 * l)
        c = _diff_attention(c_qkv, bsz, seq, row(diff_lambda_q1[l]), row(diff_lambda_k1[l]),
                            row(diff_lambda_q2[l]), row(diff_lambda_k2[l]), row(diff_subln[l]),
                            lambda_init)

        xt = _mix_out(xt, a, u, c, pool_w[l].astype(BF16), row(pool_scale[l]),
                      w_out[l].astype(BF16), seq)

        xt = _ffn(xt, row(ffn2_norm[l]), ffn2_w_gate[l].astype(BF16), ffn2_w_up[l].astype(BF16),
                  ffn2_w_down[l].astype(BF16), row(final_norm), final_norm=(l == depth - 1))
    return xt.reshape(bsz, seq, d)
```

```python
import functools
import math

import jax
import jax.numpy as jnp
from jax import lax
from jax.experimental import pallas as pl
from jax.experimental.pallas import tpu as pltpu

F32 = jnp.float32
BF16 = jnp.bfloat16

EPS = 1e-6
NEG_INF = -1e30

HEAD_DIM = 64
LANES = 128
A_HEADS = 12
A_WIDTH = A_HEADS * HEAD_DIM
A_PATTERNS = ((128, 1), (512, 4), (2048, 16))
A_BLOCK = 128
A_SUPER = A_BLOCK * max(d for _, d in A_PATTERNS)
TASKS_PER_TRIP = 16
MIX_ROWS = 256
B_WINDOWS = (2, 4, 8, 16)
B_GROUP_DIM = 128
B_WIDTH = len(B_WINDOWS) * B_GROUP_DIM
B_HALO = 16
C_HEADS = 6
C_V_DIM = 2 * HEAD_DIM
C_WIDTH = C_HEADS * C_V_DIM
QK_SCALE = HEAD_DIM ** -0.5
LOG2_E = math.log2(math.e)
BF16_ROWS = 16

VMEM_LIMIT_BYTES = 56 * 1024 * 1024


def _params(n_axes):
    return pltpu.CompilerParams(
        dimension_semantics=("arbitrary",) * n_axes,
        vmem_limit_bytes=VMEM_LIMIT_BYTES,
    )


def _rmsnorm_rows(x, g):
    ms = jnp.mean(x * x, axis=-1, keepdims=True)
    return x * lax.rsqrt(ms + EPS) * g


def _ffn_kernel(x_ref, g_ref, wg_ref, wu_ref, wd_ref, fg_ref, o_ref, h_ref, *, final_norm):
    f = pl.program_id(1)

    @pl.when(f == 0)
    def _():
        x = x_ref[...]
        h_ref[...] = _rmsnorm_rows(x, g_ref[...]).astype(BF16)
        o_ref[...] = x

    h = h_ref[...]
    gate = jnp.dot(h, wg_ref[...], preferred_element_type=F32)
    up = jnp.dot(h, wu_ref[...], preferred_element_type=F32)
    act = (gate * (1.0 / (1.0 + jnp.exp(-gate))) * up).astype(BF16)
    o_ref[...] += 0.5 * jnp.dot(act, wd_ref[...], preferred_element_type=F32)

    if final_norm:
        @pl.when(f == pl.num_programs(1) - 1)
        def _():
            o_ref[...] = _rmsnorm_rows(o_ref[...], fg_ref[...])


def _ffn(x, g, wg, wu, wd, final_g, *, final_norm, tm=1024, tf=512):
    t, d = x.shape
    d_ff = wg.shape[1]
    assert t % tm == 0 and d_ff % tf == 0
    return pl.pallas_call(
        functools.partial(_ffn_kernel, final_norm=final_norm),
        grid=(t // tm, d_ff // tf),
        in_specs=[
            pl.BlockSpec((tm, d), lambda i, f: (i, 0)),
            pl.BlockSpec((1, d), lambda i, f: (0, 0)),
            pl.BlockSpec((d, tf), lambda i, f: (0, f)),
            pl.BlockSpec((d, tf), lambda i, f: (0, f)),
            pl.BlockSpec((tf, d), lambda i, f: (f, 0)),
            pl.BlockSpec((1, d), lambda i, f: (0, 0)),
        ],
        out_specs=pl.BlockSpec((tm, d), lambda i, f: (i, 0)),
        out_shape=jax.ShapeDtypeStruct((t, d), F32),
        scratch_shapes=[pltpu.VMEM((tm, d), BF16)],
        compiler_params=_params(2),
        name="ffn_final" if final_norm else "ffn",
    )(x, g, wg, wu, wd, final_g)


PROJ_TILE = A_WIDTH
PROJ_A_TILES = 3
PROJ_C_TILES = 3


def _mix_projection_kernel(x_ref, g_ref, w_ref, a_ref, c_ref, u_ref, h_ref):
    j = pl.program_id(1)

    @pl.when(j == 0)
    def _():
        h_ref[...] = _rmsnorm_rows(x_ref[...], g_ref[...]).astype(BF16)

    def product():
        return jnp.dot(h_ref[...], w_ref[...], preferred_element_type=F32)

    @pl.when(j < PROJ_A_TILES)
    def _():
        a_ref[...] = product()

    @pl.when((j >= PROJ_A_TILES) & (j < PROJ_A_TILES + PROJ_C_TILES))
    def _():
        c_ref[...] = product().astype(BF16)

    @pl.when(j == PROJ_A_TILES + PROJ_C_TILES)
    def _():
        u_ref[...] = product()


def _mix_projection(x, g, w, *, tm=1024):
    t, d = x.shape
    tm = min(tm, t)
    n_tiles = PROJ_A_TILES + PROJ_C_TILES + 1
    assert t % tm == 0 and w.shape[1] == n_tiles * PROJ_TILE
    last_a, first_c = PROJ_A_TILES - 1, PROJ_A_TILES
    return pl.pallas_call(
        _mix_projection_kernel,
        grid=(t // tm, n_tiles),
        in_specs=[
            pl.BlockSpec((tm, d), lambda i, j: (i, 0)),
            pl.BlockSpec((1, d), lambda i, j: (0, 0)),
            pl.BlockSpec((d, PROJ_TILE), lambda i, j: (0, j)),
        ],
        out_specs=[
            pl.BlockSpec((tm, PROJ_TILE), lambda i, j: (i, jnp.minimum(j, last_a))),
            pl.BlockSpec((tm, PROJ_TILE),
                         lambda i, j: (i, jnp.clip(j - first_c, 0, PROJ_C_TILES - 1))),
            pl.BlockSpec((tm, PROJ_TILE), lambda i, j: (i, 0)),
        ],
        out_shape=[
            jax.ShapeDtypeStruct((t, PROJ_A_TILES * PROJ_TILE), F32),
            jax.ShapeDtypeStruct((t, PROJ_C_TILES * PROJ_TILE), BF16),
            jax.ShapeDtypeStruct((t, PROJ_TILE), F32),
        ],
        scratch_shapes=[pltpu.VMEM((tm, d), BF16)],
        compiler_params=_params(2),
        name="mix_proj",
    )(x, g, w)


def _head_pair_attention(q2, k2, v2, valid, low_half):
    outs, lses = [], []
    for half in range(2):
        qm = jnp.where(low_half, q2, 0) if half == 0 else jnp.where(low_half, 0, q2)
        s = lax.dot_general(qm, k2, (((1,), (1,)), ((), ())), preferred_element_type=F32)
        s = jnp.where(valid, s, NEG_INF)
        m = jnp.max(s, axis=-1, keepdims=True)
        p = jnp.exp(s - m)
        l = jnp.sum(p, axis=-1, keepdims=True)
        o = jnp.dot(p.astype(BF16), v2, preferred_element_type=F32)
        outs.append(o / l)
        lses.append(jnp.broadcast_to(m + jnp.log(l), (A_BLOCK, LANES)))
    return jnp.where(low_half, outs[0], outs[1]), jnp.where(low_half, lses[0], lses[1])


def _window_mixture_kernel(q_ref, k_ref, v_ref, a_ref, kk_ref, vv_ref, o_ref, lse_ref):
    first = pl.program_id(2) == 0

    @pl.when(first)
    def _():
        kk_ref[0:A_SUPER, :] = jnp.zeros((A_SUPER, LANES), F32)
        vv_ref[0:A_SUPER, :] = jnp.zeros((A_SUPER, LANES), F32)

    kk_ref[A_SUPER:, :] = k_ref[0]
    vv_ref[A_SUPER:, :] = v_ref[0]

    qi = lax.broadcasted_iota(jnp.int32, (A_BLOCK, 2 * A_BLOCK), 0)
    kj = lax.broadcasted_iota(jnp.int32, (A_BLOCK, 2 * A_BLOCK), 1)
    band = kj <= qi + A_BLOCK
    low_half = lax.broadcasted_iota(jnp.int32, (A_BLOCK, LANES), 1) < HEAD_DIM
    n_tasks = A_SUPER // A_BLOCK

    for p, (_, dilation) in enumerate(A_PATTERNS):
        def tasks(g, carry, p=p, dilation=dilation):
            for u in range(TASKS_PER_TRIP):
                t = g * TASKS_PER_TRIP + u
                sub, r = t // dilation, t % dilation
                start = sub * (A_BLOCK * dilation) + r
                lowest = jnp.where(first & (sub == 0), A_BLOCK, qi)
                valid = band & (kj >= lowest)
                q_rows = pl.ds(start, A_BLOCK, stride=dilation)
                kv_rows = pl.ds(A_SUPER + start - A_BLOCK * dilation, 2 * A_BLOCK, stride=dilation)
                q2 = (q_ref[0, q_rows, :] * QK_SCALE).astype(BF16)
                k2 = kk_ref[kv_rows, :].astype(BF16)
                v2 = vv_ref[kv_rows, :].astype(BF16)
                o_ref[p, q_rows, :], lse_ref[p, q_rows, :] = _head_pair_attention(
                    q2, k2, v2, valid, low_half)
            return carry

        lax.fori_loop(0, n_tasks // TASKS_PER_TRIP, tasks, 0)

    kk_ref[0:A_SUPER, :] = k_ref[0]
    vv_ref[0:A_SUPER, :] = v_ref[0]

    def mixture(c, carry):
        rows = pl.ds(pl.multiple_of(c * MIX_ROWS, MIX_ROWS), MIX_ROWS)
        lses = [lse_ref[p, rows, :] for p in range(len(A_PATTERNS))]
        mx = functools.reduce(jnp.maximum, lses)
        es = [jnp.exp(l - mx) for l in lses]
        den = functools.reduce(lambda x, y: x + y, es)
        a = functools.reduce(lambda x, y: x + y,
                             [(e / den) * o_ref[p, rows, :] for p, e in enumerate(es)])
        a_ref[0, rows, :] = a.astype(a_ref.dtype)
        return carry

    lax.fori_loop(0, A_SUPER // MIX_ROWS, mixture, 0)


def _window_mixture(a_qkv, bsz, seq):
    assert seq % A_SUPER == 0 and all(w // d == A_BLOCK for w, d in A_PATTERNS)
    view = a_qkv.reshape(bsz, seq, 3 * A_WIDTH)
    per_part = A_WIDTH // LANES
    part = lambda p: pl.BlockSpec((1, A_SUPER, LANES), lambda b, hp, i: (b, i, p * per_part + hp))
    n_pat = len(A_PATTERNS)
    out = pl.pallas_call(
        _window_mixture_kernel,
        grid=(bsz, per_part, seq // A_SUPER),
        in_specs=[part(0), part(1), part(2)],
        out_specs=part(0),
        out_shape=jax.ShapeDtypeStruct((bsz, seq, A_WIDTH), BF16),
        scratch_shapes=[
            pltpu.VMEM((2 * A_SUPER, LANES), F32),
            pltpu.VMEM((2 * A_SUPER, LANES), F32),
            pltpu.VMEM((n_pat, A_SUPER, LANES), F32),
            pltpu.VMEM((n_pat, A_SUPER, LANES), F32),
        ],
        compiler_params=_params(3),
        name="window_mixture",
    )(view, view, view)
    return out.reshape(bsz * seq, A_WIDTH)


def _diff_attn_kernel(q_ref, k_ref, v_ref, lq1_ref, lk1_ref, lq2_ref, lk2_ref, g_ref, o_ref,
                      qt_ref, vt_ref, s_ref, cmax_ref, m_ref, acc_ref, *, tq, tk, lambda_init):
    qb = pl.program_id(2)
    seq = v_ref.shape[1]

    @pl.when(qb == 0)
    def _():
        def transpose_rows(c, carry):
            off = pl.multiple_of(c * tq, tq)
            vt_ref[0:C_V_DIM, pl.ds(off, tq)] = (
                v_ref[0, pl.ds(off, tq), :].astype(F32).T.astype(BF16))
            return carry
        lax.fori_loop(0, seq // tq, transpose_rows, 0)
        vt_ref[C_V_DIM:, :] = jnp.ones((vt_ref.shape[0] - C_V_DIM, seq), BF16)

    qt = (q_ref[0].astype(F32) * (QK_SCALE * LOG2_E)).T.astype(BF16)
    dim = lax.broadcasted_iota(jnp.int32, (2 * HEAD_DIM, tq), 0)
    zeros = jnp.zeros_like(qt)
    qt_ref[0] = jnp.where(dim < HEAD_DIM, qt, zeros)
    qt_ref[1] = jnp.where(dim < HEAD_DIM, zeros, qt)
    m_ref[...] = jnp.full(m_ref.shape, NEG_INF, F32)
    acc_ref[...] = jnp.zeros(acc_ref.shape, F32)

    def score(slot, key_off, diag_chunk):
        k = k_ref[0, pl.ds(pl.multiple_of(key_off, tk), tk), :]
        for mi in range(2):
            st = jnp.dot(k, qt_ref[mi], preferred_element_type=F32)
            if diag_chunk is not None:
                key = lax.broadcasted_iota(jnp.int32, (tk, tq), 0) + diag_chunk * tk
                qry = lax.broadcasted_iota(jnp.int32, (tk, tq), 1)
                st = jnp.where(key <= qry, st, NEG_INF)
            s_ref[slot, mi] = st
            cmax_ref[slot, mi] = jnp.max(st, axis=0, keepdims=True)

    def absorb(slot, key_off):
        vt = vt_ref[:, pl.ds(pl.multiple_of(key_off, tk), tk)]
        for mi in range(2):
            m_old = m_ref[mi]
            m_new = jnp.maximum(m_old, cmax_ref[slot, mi])
            alpha = jnp.exp2(m_old - m_new)
            pt = jnp.exp2(s_ref[slot, mi] - m_new)
            m_ref[mi] = m_new
            pv = jnp.dot(vt, pt.astype(BF16), preferred_element_type=F32)
            acc_ref[mi] = alpha * acc_ref[mi] + pv

    diag = qb * tq
    score(0, diag, 0)

    def two_blocks(t, carry):
        off = 2 * t * tq
        score(1, off, None)
        absorb(0, jnp.where(t == 0, diag, off - tq))
        score(0, off + tq, None)
        absorb(1, off)
        return carry

    pairs = qb // 2
    lax.fori_loop(0, pairs, two_blocks, 0)
    pending = jnp.where(pairs == 0, diag, 2 * pairs * tq - tq)

    @pl.when(qb % 2 == 1)
    def _():
        score(1, diag - tq, None)
        absorb(0, pending)
        absorb(1, diag - tq)

    @pl.when(qb % 2 == 0)
    def _():
        absorb(0, pending)

    lam = (jnp.exp(jnp.sum(lq1_ref[...] * lk1_ref[...], axis=-1, keepdims=True))
           - jnp.exp(jnp.sum(lq2_ref[...] * lk2_ref[...], axis=-1, keepdims=True))
           + lambda_init)
    l1 = acc_ref[0, C_V_DIM:C_V_DIM + 1, :]
    l2 = acc_ref[1, C_V_DIM:C_V_DIM + 1, :]
    ot = acc_ref[0, 0:C_V_DIM, :] / l1 - lam * (acc_ref[1, 0:C_V_DIM, :] / l2)
    ms = jnp.mean(ot * ot, axis=0, keepdims=True)
    ot = ot * lax.rsqrt(ms + EPS) * g_ref[...] * (1.0 - lambda_init)
    o_ref[0] = ot.T.astype(o_ref.dtype)


def _diff_attention(proj, bsz, seq, lq1, lk1, lq2, lk2, subln_g, lambda_init, *, tq=1024):
    tq = min(tq, seq)
    tk = tq
    assert seq % tq == 0
    cols = proj.shape[-1]
    view = proj.reshape(bsz, seq, cols)
    q0 = 0
    k0 = q0 + C_HEADS
    v0 = k0 + C_HEADS
    vec = lambda: pl.BlockSpec((1, HEAD_DIM), lambda b, h, i: (0, 0))
    out = pl.pallas_call(
        functools.partial(_diff_attn_kernel, tq=tq, tk=tk, lambda_init=lambda_init),
        grid=(bsz, C_HEADS, seq // tq),
        in_specs=[
            pl.BlockSpec((1, tq, LANES), lambda b, h, i: (b, i, q0 + h)),
            pl.BlockSpec((1, seq, LANES), lambda b, h, i: (b, 0, k0 + h)),
            pl.BlockSpec((1, seq, LANES), lambda b, h, i: (b, 0, v0 + h)),
            vec(), vec(), vec(), vec(),
            pl.BlockSpec((C_V_DIM, 1), lambda b, h, i: (0, 0)),
        ],
        out_specs=pl.BlockSpec((1, tq, LANES), lambda b, h, i: (b, i, h)),
        out_shape=jax.ShapeDtypeStruct((bsz, seq, C_WIDTH), BF16),
        scratch_shapes=[
            pltpu.VMEM((2, C_V_DIM, tq), BF16),
            pltpu.VMEM((C_V_DIM + BF16_ROWS, seq), BF16),
            pltpu.VMEM((2, 2, tk, tq), F32),
            pltpu.VMEM((2, 2, 1, tq), F32),
            pltpu.VMEM((2, 1, tq), F32),
            pltpu.VMEM((2, C_V_DIM + BF16_ROWS, tq), F32),
        ],
        compiler_params=_params(3),
        name="diff_attn",
    )(view, view, view, lq1, lk1, lq2, lk2, subln_g.reshape(C_V_DIM, 1))
    return out.reshape(bsz * seq, C_WIDTH)


def _mix_out_kernel(x_ref, a_ref, u_ref, halo_ref, c_ref, pw_ref, ps_ref, wo_ref, out_ref,
                    *, tm, rows_per_seq):
    i = pl.program_id(0)

    acc = jnp.dot(a_ref[...], wo_ref[0:A_WIDTH, :], preferred_element_type=F32)

    t_in_seq = (i * tm) % rows_per_seq + lax.broadcasted_iota(jnp.int32, (tm, 1), 0)
    first = (i * tm) % rows_per_seq == 0
    halo = jnp.where(first, jnp.zeros((B_HALO, B_WIDTH), F32), halo_ref[...])
    u = u_ref[...]
    ext = jnp.concatenate([halo, u], axis=0)
    ys = []
    win = ext
    width = 1
    for g, w in enumerate(B_WINDOWS):
        while width < w:
            win = win + pltpu.roll(win, width, 0)
            width *= 2
        gs = slice(g * B_GROUP_DIM, (g + 1) * B_GROUP_DIM)
        cnt = jnp.minimum(t_in_seq + 1, w).astype(F32)
        diff = win[B_HALO:, gs] / cnt - u[:, gs]
        ys.append(jnp.dot(diff.astype(BF16), pw_ref[g], preferred_element_type=F32))
    b = jnp.concatenate(ys, axis=-1) * ps_ref[...]
    acc += jnp.dot(b.astype(BF16), wo_ref[A_WIDTH:A_WIDTH + B_WIDTH, :], preferred_element_type=F32)

    acc += jnp.dot(c_ref[...], wo_ref[A_WIDTH + B_WIDTH:, :], preferred_element_type=F32)
    out_ref[...] = x_ref[...] + acc


def _mix_out(x, a, u, c, pool_w, pool_scale, w_out, seq, *, tm=256):
    t, d = x.shape
    assert t % tm == 0 and seq % tm == 0 and tm % B_HALO == 0
    row = lambda width: pl.BlockSpec((tm, width), lambda i: (i, 0))
    halo_blocks = tm // B_HALO
    return pl.pallas_call(
        functools.partial(_mix_out_kernel, tm=tm, rows_per_seq=seq),
        grid=(t // tm,),
        in_specs=[
            row(d),
            row(A_WIDTH),
            row(B_WIDTH),
            pl.BlockSpec((B_HALO, B_WIDTH), lambda i: (jnp.maximum(i * halo_blocks - 1, 0), 0)),
            row(C_WIDTH),
            pl.BlockSpec(pool_w.shape, lambda i: (0, 0, 0)),
            pl.BlockSpec((1, B_WIDTH), lambda i: (0, 0)),
            pl.BlockSpec(w_out.shape, lambda i: (0, 0)),
        ],
        out_specs=row(d),
        out_shape=jax.ShapeDtypeStruct((t, d), F32),
        compiler_params=_params(1),
        name="mix_out",
    )(x, a, u, u, c, pool_w, pool_scale, w_out)


def kernel(x, ffn1_norm, ffn1_w_gate, ffn1_w_up, ffn1_w_down, mix_norm, w_in, pool_w, pool_scale,
           diff_lambda_q1, diff_lambda_k1, diff_lambda_q2, diff_lambda_k2, diff_subln, w_out,
           ffn2_norm, ffn2_w_gate, ffn2_w_up, ffn2_w_down, final_norm):
    bsz, seq, d = x.shape
    depth = w_in.shape[0]
    b_lo, c_lo = 3 * A_WIDTH, 3 * A_WIDTH + B_WIDTH
    assert w_in.shape[2] == c_lo + 3 * C_WIDTH and d == A_WIDTH + B_WIDTH + C_WIDTH
    xt = x.reshape(bsz * seq, d)
    row = lambda v: v.reshape(1, -1).astype(F32)
    for l in range(depth):
        xt = _ffn(xt, row(ffn1_norm[l]), ffn1_w_gate[l].astype(BF16), ffn1_w_up[l].astype(BF16),
                  ffn1_w_down[l].astype(BF16), row(final_norm), final_norm=False)

        w_l = w_in[l].astype(BF16)
        w_mix = jnp.concatenate([w_l[:, :b_lo], w_l[:, c_lo:], w_l[:, b_lo:c_lo],
                                 jnp.zeros((d, PROJ_TILE - B_WIDTH), BF16)], axis=1)
        a_qkv, c_qkv, u = _mix_projection(xt, row(mix_norm[l]), w_mix)

        a = _window_mixture(a_qkv, bsz, seq)

        lambda_init = 0.8 - 0.6 * math.exp(-0.3 * l)
        c = _diff_attention(c_qkv, bsz, seq, row(diff_lambda_q1[l]), row(diff_lambda_k1[l]),
                            row(diff_lambda_q2[l]), row(diff_lambda_k2[l]), row(diff_subln[l]),
                            lambda_init)

        xt = _mix_out(xt, a, u, c, pool_w[l].astype(BF16), row(pool_scale[l]),
                      w_out[l].astype(BF16), seq)

        xt = _ffn(xt, row(ffn2_norm[l]), ffn2_w_gate[l].astype(BF16), ffn2_w_up[l].astype(BF16),
                  ffn2_w_down[l].astype(BF16), row(final_norm), final_norm=(l == depth - 1))
    return xt.reshape(bsz, seq, d)
```

```python
import functools
import math

import jax
import jax.numpy as jnp
from jax import lax
from jax.experimental import pallas as pl
from jax.experimental.pallas import tpu as pltpu

F32 = jnp.float32
BF16 = jnp.bfloat16

EPS = 1e-6
NEG_INF = -1e30

HEAD_DIM = 64
LANES = 128
A_HEADS = 12
A_WIDTH = A_HEADS * HEAD_DIM
A_PATTERNS = ((128, 1), (512, 4), (2048, 16))
A_BLOCK = 128
A_SUPER = A_BLOCK * max(d for _, d in A_PATTERNS)
TASKS_PER_TRIP = 16
MIX_ROWS = 256
B_WINDOWS = (2, 4, 8, 16)
B_GROUP_DIM = 128
B_WIDTH = len(B_WINDOWS) * B_GROUP_DIM
B_HALO = 16
C_HEADS = 6
C_V_DIM = 2 * HEAD_DIM
C_WIDTH = C_HEADS * C_V_DIM
QK_SCALE = HEAD_DIM ** -0.5
LOG2_E = math.log2(math.e)
BF16_ROWS = 16

VMEM_LIMIT_BYTES = 56 * 1024 * 1024


def _params(n_axes):
    return pltpu.CompilerParams(
        dimension_semantics=("arbitrary",) * n_axes,
        vmem_limit_bytes=VMEM_LIMIT_BYTES,
    )


def _rmsnorm_rows(x, g):
    ms = jnp.mean(x * x, axis=-1, keepdims=True)
    return x * lax.rsqrt(ms + EPS) * g


def _ffn_kernel(x_ref, g_ref, wg_ref, wu_ref, wd_ref, fg_ref, o_ref, h_ref,
                *, n_ffn, chunks, final_norm):
    step = pl.program_id(1)

    @pl.when(step == 0)
    def _():
        x = x_ref[...]
        h_ref[...] = _rmsnorm_rows(x, g_ref[0:1, :]).astype(BF16)
        o_ref[...] = x

    for e in range(1, n_ffn):
        @pl.when(step == e * chunks)
        def _():
            h_ref[...] = _rmsnorm_rows(o_ref[...], g_ref[e:e + 1, :]).astype(BF16)

    h = h_ref[...]
    gate = jnp.dot(h, wg_ref[...], preferred_element_type=F32)
    up = jnp.dot(h, wu_ref[...], preferred_element_type=F32)
    act = (gate * (1.0 / (1.0 + jnp.exp(-gate))) * up).astype(BF16)
    o_ref[...] += 0.5 * jnp.dot(act, wd_ref[...], preferred_element_type=F32)

    if final_norm:
        @pl.when(step == n_ffn * chunks - 1)
        def _():
            o_ref[...] = _rmsnorm_rows(o_ref[...], fg_ref[...])


def _ffn(x, gains, wg, wu, wd, final_g, slots, *, final_norm, tm=1024, tf=512):
    t, d = x.shape
    d_ff = wg.shape[2]
    n_ffn = len(slots)
    assert t % tm == 0 and d_ff % tf == 0 and gains.shape == (n_ffn, d) and n_ffn in (1, 2)
    chunks = d_ff // tf
    first, step_up = slots[0], slots[-1] - slots[0]
    slot = lambda s: first + (s // chunks) * step_up
    return pl.pallas_call(
        functools.partial(_ffn_kernel, n_ffn=n_ffn, chunks=chunks, final_norm=final_norm),
        grid=(t // tm, n_ffn * chunks),
        in_specs=[
            pl.BlockSpec((tm, d), lambda i, s: (i, 0)),
            pl.BlockSpec((n_ffn, d), lambda i, s: (0, 0)),
            pl.BlockSpec((None, d, tf), lambda i, s: (slot(s), 0, s % chunks)),
            pl.BlockSpec((None, d, tf), lambda i, s: (slot(s), 0, s % chunks)),
            pl.BlockSpec((None, tf, d), lambda i, s: (slot(s), s % chunks, 0)),
            pl.BlockSpec((1, d), lambda i, s: (0, 0)),
        ],
        out_specs=pl.BlockSpec((tm, d), lambda i, s: (i, 0)),
        out_shape=jax.ShapeDtypeStruct((t, d), F32),
        scratch_shapes=[pltpu.VMEM((tm, d), BF16)],
        compiler_params=_params(2),
        name="ffn_final" if final_norm else f"ffn_x{n_ffn}",
    )(x, gains, wg, wu, wd, final_g)


PROJ_TILE = A_WIDTH
PROJ_A_TILES = 3
PROJ_C_TILES = 3


def _mix_projection_kernel(x_ref, g_ref, w_ref, a_ref, c_ref, u_ref, h_ref):
    j = pl.program_id(1)

    @pl.when(j == 0)
    def _():
        h_ref[...] = _rmsnorm_rows(x_ref[...], g_ref[...]).astype(BF16)

    def product():
        return jnp.dot(h_ref[...], w_ref[...], preferred_element_type=F32)

    @pl.when(j < PROJ_A_TILES)
    def _():
        a_ref[...] = product()

    @pl.when((j >= PROJ_A_TILES) & (j < PROJ_A_TILES + PROJ_C_TILES))
    def _():
        c_ref[...] = product().astype(BF16)

    @pl.when(j == PROJ_A_TILES + PROJ_C_TILES)
    def _():
        u_ref[...] = product()


def _mix_projection(x, g, w, *, tm=1024):
    t, d = x.shape
    tm = min(tm, t)
    n_tiles = PROJ_A_TILES + PROJ_C_TILES + 1
    assert t % tm == 0 and w.shape[1] == n_tiles * PROJ_TILE
    last_a, first_c = PROJ_A_TILES - 1, PROJ_A_TILES
    return pl.pallas_call(
        _mix_projection_kernel,
        grid=(t // tm, n_tiles),
        in_specs=[
            pl.BlockSpec((tm, d), lambda i, j: (i, 0)),
            pl.BlockSpec((1, d), lambda i, j: (0, 0)),
            pl.BlockSpec((d, PROJ_TILE), lambda i, j: (0, j)),
        ],
        out_specs=[
            pl.BlockSpec((tm, PROJ_TILE), lambda i, j: (i, jnp.minimum(j, last_a))),
            pl.BlockSpec((tm, PROJ_TILE),
                         lambda i, j: (i, jnp.clip(j - first_c, 0, PROJ_C_TILES - 1))),
            pl.BlockSpec((tm, PROJ_TILE), lambda i, j: (i, 0)),
        ],
        out_shape=[
            jax.ShapeDtypeStruct((t, PROJ_A_TILES * PROJ_TILE), F32),
            jax.ShapeDtypeStruct((t, PROJ_C_TILES * PROJ_TILE), BF16),
            jax.ShapeDtypeStruct((t, PROJ_TILE), F32),
        ],
        scratch_shapes=[pltpu.VMEM((tm, d), BF16)],
        compiler_params=_params(2),
        name="mix_proj",
    )(x, g, w)


def _head_pair_attention(q2, k2, v2, valid, low_half):
    outs, lses = [], []
    for half in range(2):
        qm = jnp.where(low_half, q2, 0) if half == 0 else jnp.where(low_half, 0, q2)
        s = lax.dot_general(qm, k2, (((1,), (1,)), ((), ())), preferred_element_type=F32)
        s = jnp.where(valid, s, NEG_INF)
        m = jnp.max(s, axis=-1, keepdims=True)
        p = jnp.exp(s - m)
        l = jnp.sum(p, axis=-1, keepdims=True)
        o = jnp.dot(p.astype(BF16), v2, preferred_element_type=F32)
        outs.append(o / l)
        lses.append(jnp.broadcast_to(m + jnp.log(l), (A_BLOCK, LANES)))
    return jnp.where(low_half, outs[0], outs[1]), jnp.where(low_half, lses[0], lses[1])


def _window_mixture_kernel(q_ref, k_ref, v_ref, a_ref, kk_ref, vv_ref, o_ref, lse_ref):
    first = pl.program_id(2) == 0

    @pl.when(first)
    def _():
        kk_ref[0:A_SUPER, :] = jnp.zeros((A_SUPER, LANES), F32)
        vv_ref[0:A_SUPER, :] = jnp.zeros((A_SUPER, LANES), F32)

    kk_ref[A_SUPER:, :] = k_ref[0]
    vv_ref[A_SUPER:, :] = v_ref[0]

    qi = lax.broadcasted_iota(jnp.int32, (A_BLOCK, 2 * A_BLOCK), 0)
    kj = lax.broadcasted_iota(jnp.int32, (A_BLOCK, 2 * A_BLOCK), 1)
    band = kj <= qi + A_BLOCK
    low_half = lax.broadcasted_iota(jnp.int32, (A_BLOCK, LANES), 1) < HEAD_DIM
    n_tasks = A_SUPER // A_BLOCK

    for p, (_, dilation) in enumerate(A_PATTERNS):
        def tasks(g, carry, p=p, dilation=dilation):
            for u in range(TASKS_PER_TRIP):
                t = g * TASKS_PER_TRIP + u
                sub, r = t // dilation, t % dilation
                start = sub * (A_BLOCK * dilation) + r
                lowest = jnp.where(first & (sub == 0), A_BLOCK, qi)
                valid = band & (kj >= lowest)
                q_rows = pl.ds(start, A_BLOCK, stride=dilation)
                kv_rows = pl.ds(A_SUPER + start - A_BLOCK * dilation, 2 * A_BLOCK, stride=dilation)
                q2 = (q_ref[0, q_rows, :] * QK_SCALE).astype(BF16)
                k2 = kk_ref[kv_rows, :].astype(BF16)
                v2 = vv_ref[kv_rows, :].astype(BF16)
                o_ref[p, q_rows, :], lse_ref[p, q_rows, :] = _head_pair_attention(
                    q2, k2, v2, valid, low_half)
            return carry

        lax.fori_loop(0, n_tasks // TASKS_PER_TRIP, tasks, 0)

    kk_ref[0:A_SUPER, :] = k_ref[0]
    vv_ref[0:A_SUPER, :] = v_ref[0]

    def mixture(c, carry):
        rows = pl.ds(pl.multiple_of(c * MIX_ROWS, MIX_ROWS), MIX_ROWS)
        lses = [lse_ref[p, rows, :] for p in range(len(A_PATTERNS))]
        mx = functools.reduce(jnp.maximum, lses)
        es = [jnp.exp(l - mx) for l in lses]
        den = functools.reduce(lambda x, y: x + y, es)
        a = functools.reduce(lambda x, y: x + y,
                             [(e / den) * o_ref[p, rows, :] for p, e in enumerate(es)])
        a_ref[0, rows, :] = a.astype(a_ref.dtype)
        return carry

    lax.fori_loop(0, A_SUPER // MIX_ROWS, mixture, 0)


def _window_mixture(a_qkv, bsz, seq):
    assert seq % A_SUPER == 0 and all(w // d == A_BLOCK for w, d in A_PATTERNS)
    view = a_qkv.reshape(bsz, seq, 3 * A_WIDTH)
    per_part = A_WIDTH // LANES
    part = lambda p: pl.BlockSpec((1, A_SUPER, LANES), lambda b, hp, i: (b, i, p * per_part + hp))
    n_pat = len(A_PATTERNS)
    out = pl.pallas_call(
        _window_mixture_kernel,
        grid=(bsz, per_part, seq // A_SUPER),
        in_specs=[part(0), part(1), part(2)],
        out_specs=part(0),
        out_shape=jax.ShapeDtypeStruct((bsz, seq, A_WIDTH), BF16),
        scratch_shapes=[
            pltpu.VMEM((2 * A_SUPER, LANES), F32),
            pltpu.VMEM((2 * A_SUPER, LANES), F32),
            pltpu.VMEM((n_pat, A_SUPER, LANES), F32),
            pltpu.VMEM((n_pat, A_SUPER, LANES), F32),
        ],
        compiler_params=_params(3),
        name="window_mixture",
    )(view, view, view)
    return out.reshape(bsz * seq, A_WIDTH)


def _diff_attn_kernel(q_ref, k_ref, v_ref, lq1_ref, lk1_ref, lq2_ref, lk2_ref, g_ref, o_ref,
                      qt_ref, vt_ref, s_ref, cmax_ref, m_ref, acc_ref, *, tq, tk, lambda_init):
    qb = pl.program_id(2)
    seq = v_ref.shape[1]

    @pl.when(qb == 0)
    def _():
        def transpose_rows(c, carry):
            off = pl.multiple_of(c * tq, tq)
            vt_ref[0:C_V_DIM, pl.ds(off, tq)] = (
                v_ref[0, pl.ds(off, tq), :].astype(F32).T.astype(BF16))
            return carry
        lax.fori_loop(0, seq // tq, transpose_rows, 0)
        vt_ref[C_V_DIM:, :] = jnp.ones((vt_ref.shape[0] - C_V_DIM, seq), BF16)

    qt = (q_ref[0].astype(F32) * (QK_SCALE * LOG2_E)).T.astype(BF16)
    dim = lax.broadcasted_iota(jnp.int32, (2 * HEAD_DIM, tq), 0)
    zeros = jnp.zeros_like(qt)
    qt_ref[0] = jnp.where(dim < HEAD_DIM, qt, zeros)
    qt_ref[1] = jnp.where(dim < HEAD_DIM, zeros, qt)
    m_ref[...] = jnp.full(m_ref.shape, NEG_INF, F32)
    acc_ref[...] = jnp.zeros(acc_ref.shape, F32)

    def score(slot, key_off, diag_chunk):
        k = k_ref[0, pl.ds(pl.multiple_of(key_off, tk), tk), :]
        for mi in range(2):
            st = jnp.dot(k, qt_ref[mi], preferred_element_type=F32)
            if diag_chunk is not None:
                key = lax.broadcasted_iota(jnp.int32, (tk, tq), 0) + diag_chunk * tk
                qry = lax.broadcasted_iota(jnp.int32, (tk, tq), 1)
                st = jnp.where(key <= qry, st, NEG_INF)
            s_ref[slot, mi] = st
            cmax_ref[slot, mi] = jnp.max(st, axis=0, keepdims=True)

    def absorb(slot, key_off):
        vt = vt_ref[:, pl.ds(pl.multiple_of(key_off, tk), tk)]
        for mi in range(2):
            m_old = m_ref[mi]
            m_new = jnp.maximum(m_old, cmax_ref[slot, mi])
            alpha = jnp.exp2(m_old - m_new)
            pt = jnp.exp2(s_ref[slot, mi] - m_new)
            m_ref[mi] = m_new
            pv = jnp.dot(vt, pt.astype(BF16), preferred_element_type=F32)
            acc_ref[mi] = alpha * acc_ref[mi] + pv

    diag = qb * tq
    score(0, diag, 0)

    def two_blocks(t, carry):
        off = 2 * t * tq
        score(1, off, None)
        absorb(0, jnp.where(t == 0, diag, off - tq))
        score(0, off + tq, None)
        absorb(1, off)
        return carry

    pairs = qb // 2
    lax.fori_loop(0, pairs, two_blocks, 0)
    pending = jnp.where(pairs == 0, diag, 2 * pairs * tq - tq)

    @pl.when(qb % 2 == 1)
    def _():
        score(1, diag - tq, None)
        absorb(0, pending)
        absorb(1, diag - tq)

    @pl.when(qb % 2 == 0)
    def _():
        absorb(0, pending)

    lam = (jnp.exp(jnp.sum(lq1_ref[...] * lk1_ref[...], axis=-1, keepdims=True))
           - jnp.exp(jnp.sum(lq2_ref[...] * lk2_ref[...], axis=-1, keepdims=True))
           + lambda_init)
    l1 = acc_ref[0, C_V_DIM:C_V_DIM + 1, :]
    l2 = acc_ref[1, C_V_DIM:C_V_DIM + 1, :]
    ot = acc_ref[0, 0:C_V_DIM, :] / l1 - lam * (acc_ref[1, 0:C_V_DIM, :] / l2)
    ms = jnp.mean(ot * ot, axis=0, keepdims=True)
    ot = ot * lax.rsqrt(ms + EPS) * g_ref[...] * (1.0 - lambda_init)
    o_ref[0] = ot.T.astype(o_ref.dtype)


def _diff_attention(proj, bsz, seq, lq1, lk1, lq2, lk2, subln_g, lambda_init, *, tq=1024):
    tq = min(tq, seq)
    tk = tq
    assert seq % tq == 0
    cols = proj.shape[-1]
    view = proj.reshape(bsz, seq, cols)
    q0 = 0
    k0 = q0 + C_HEADS
    v0 = k0 + C_HEADS
    vec = lambda: pl.BlockSpec((1, HEAD_DIM), lambda b, h, i: (0, 0))
    out = pl.pallas_call(
        functools.partial(_diff_attn_kernel, tq=tq, tk=tk, lambda_init=lambda_init),
        grid=(bsz, C_HEADS, seq // tq),
        in_specs=[
            pl.BlockSpec((1, tq, LANES), lambda b, h, i: (b, i, q0 + h)),
            pl.BlockSpec((1, seq, LANES), lambda b, h, i: (b, 0, k0 + h)),
            pl.BlockSpec((1, seq, LANES), lambda b, h, i: (b, 0, v0 + h)),
            vec(), vec(), vec(), vec(),
            pl.BlockSpec((C_V_DIM, 1), lambda b, h, i: (0, 0)),
        ],
        out_specs=pl.BlockSpec((1, tq, LANES), lambda b, h, i: (b, i, h)),
        out_shape=jax.ShapeDtypeStruct((bsz, seq, C_WIDTH), BF16),
        scratch_shapes=[
            pltpu.VMEM((2, C_V_DIM, tq), BF16),
            pltpu.VMEM((C_V_DIM + BF16_ROWS, seq), BF16),
            pltpu.VMEM((2, 2, tk, tq), F32),
            pltpu.VMEM((2, 2, 1, tq), F32),
            pltpu.VMEM((2, 1, tq), F32),
            pltpu.VMEM((2, C_V_DIM + BF16_ROWS, tq), F32),
        ],
        compiler_params=_params(3),
        name="diff_attn",
    )(view, view, view, lq1, lk1, lq2, lk2, subln_g.reshape(C_V_DIM, 1))
    return out.reshape(bsz * seq, C_WIDTH)


def _mix_out_kernel(x_ref, a_ref, u_ref, halo_ref, c_ref, pw_ref, ps_ref, wo_ref, out_ref,
                    *, tm, rows_per_seq):
    i = pl.program_id(0)

    acc = jnp.dot(a_ref[...], wo_ref[0:A_WIDTH, :], preferred_element_type=F32)

    t_in_seq = (i * tm) % rows_per_seq + lax.broadcasted_iota(jnp.int32, (tm, 1), 0)
    first = (i * tm) % rows_per_seq == 0
    halo = jnp.where(first, jnp.zeros((B_HALO, B_WIDTH), F32), halo_ref[...])
    u = u_ref[...]
    ext = jnp.concatenate([halo, u], axis=0)
    ys = []
    win = ext
    width = 1
    for g, w in enumerate(B_WINDOWS):
        while width < w:
            win = win + pltpu.roll(win, width, 0)
            width *= 2
        gs = slice(g * B_GROUP_DIM, (g + 1) * B_GROUP_DIM)
        cnt = jnp.minimum(t_in_seq + 1, w).astype(F32)
        diff = win[B_HALO:, gs] / cnt - u[:, gs]
        ys.append(jnp.dot(diff.astype(BF16), pw_ref[g], preferred_element_type=F32))
    b = jnp.concatenate(ys, axis=-1) * ps_ref[...]
    acc += jnp.dot(b.astype(BF16), wo_ref[A_WIDTH:A_WIDTH + B_WIDTH, :], preferred_element_type=F32)

    acc += jnp.dot(c_ref[...], wo_ref[A_WIDTH + B_WIDTH:, :], preferred_element_type=F32)
    out_ref[...] = x_ref[...] + acc


def _mix_out(x, a, u, c, pool_w, pool_scale, w_out, seq, *, tm=256):
    t, d = x.shape
    assert t % tm == 0 and seq % tm == 0 and tm % B_HALO == 0
    row = lambda width: pl.BlockSpec((tm, width), lambda i: (i, 0))
    halo_blocks = tm // B_HALO
    return pl.pallas_call(
        functools.partial(_mix_out_kernel, tm=tm, rows_per_seq=seq),
        grid=(t // tm,),
        in_specs=[
            row(d),
            row(A_WIDTH),
            row(B_WIDTH),
            pl.BlockSpec((B_HALO, B_WIDTH), lambda i: (jnp.maximum(i * halo_blocks - 1, 0), 0)),
            row(C_WIDTH),
            pl.BlockSpec(pool_w.shape, lambda i: (0, 0, 0)),
            pl.BlockSpec((1, B_WIDTH), lambda i: (0, 0)),
            pl.BlockSpec(w_out.shape, lambda i: (0, 0)),
        ],
        out_specs=row(d),
        out_shape=jax.ShapeDtypeStruct((t, d), F32),
        compiler_params=_params(1),
        name="mix_out",
    )(x, a, u, u, c, pool_w, pool_scale, w_out)


def kernel(x, ffn1_norm, ffn1_w_gate, ffn1_w_up, ffn1_w_down, mix_norm, w_in, pool_w, pool_scale,
           diff_lambda_q1, diff_lambda_k1, diff_lambda_q2, diff_lambda_k2, diff_subln, w_out,
           ffn2_norm, ffn2_w_gate, ffn2_w_up, ffn2_w_down, final_norm):
    bsz, seq, d = x.shape
    depth = w_in.shape[0]
    b_lo, c_lo = 3 * A_WIDTH, 3 * A_WIDTH + B_WIDTH
    assert w_in.shape[2] == c_lo + 3 * C_WIDTH and d == A_WIDTH + B_WIDTH + C_WIDTH
    xt = x.reshape(bsz * seq, d)
    row = lambda v: v.reshape(1, -1).astype(F32)
    stack = lambda a, b: jnp.concatenate([a, b], axis=0).astype(BF16)
    ffn_w = (stack(ffn1_w_gate, ffn2_w_gate), stack(ffn1_w_up, ffn2_w_up),
             stack(ffn1_w_down, ffn2_w_down))
    ffn_g = jnp.concatenate([ffn1_norm, ffn2_norm], axis=0).astype(F32)
    xt = _ffn(xt, ffn_g[0:1], *ffn_w, row(final_norm), (0,), final_norm=False)
    for l in range(depth):

        w_l = w_in[l].astype(BF16)
        w_mix = jnp.concatenate([w_l[:, :b_lo], w_l[:, c_lo:], w_l[:, b_lo:c_lo],
                                 jnp.zeros((d, PROJ_TILE - B_WIDTH), BF16)], axis=1)
        a_qkv, c_qkv, u = _mix_projection(xt, row(mix_norm[l]), w_mix)

        a = _window_mixture(a_qkv, bsz, seq)

        lambda_init = 0.8 - 0.6 * math.exp(-0.3 * l)
        c = _diff_attention(c_qkv, bsz, seq, row(diff_lambda_q1[l]), row(diff_lambda_k1[l]),
                            row(diff_lambda_q2[l]), row(diff_lambda_k2[l]), row(diff_subln[l]),
                            lambda_init)

        xt = _mix_out(xt, a, u, c, pool_w[l].astype(BF16), row(pool_scale[l]),
                      w_out[l].astype(BF16), seq)

        last = l == depth - 1
        slots = (depth + l,) if last else (depth + l, l + 1)
        gains = jnp.stack([ffn_g[s] for s in slots])
        xt = _ffn(xt, gains, *ffn_w, row(final_norm), slots, final_norm=last)
    return xt.reshape(bsz, seq, d)
```

```python
import functools
import math

import jax
import jax.numpy as jnp
from jax import lax
from jax.experimental import pallas as pl
from jax.experimental.pallas import tpu as pltpu

F32 = jnp.float32
BF16 = jnp.bfloat16

EPS = 1e-6
NEG_INF = -1e30

HEAD_DIM = 64
LANES = 128
A_HEADS = 12
A_WIDTH = A_HEADS * HEAD_DIM
A_PATTERNS = ((128, 1), (512, 4), (2048, 16))
A_BLOCK = 128
A_SUPER = A_BLOCK * max(d for _, d in A_PATTERNS)
TASKS_PER_TRIP = 16
MIX_ROWS = 256
B_WINDOWS = (2, 4, 8, 16)
B_GROUP_DIM = 128
B_WIDTH = len(B_WINDOWS) * B_GROUP_DIM
B_HALO = 16
C_HEADS = 6
C_V_DIM = 2 * HEAD_DIM
C_WIDTH = C_HEADS * C_V_DIM
QK_SCALE = HEAD_DIM ** -0.5
LOG2_E = math.log2(math.e)
BF16_ROWS = 16

VMEM_LIMIT_BYTES = 56 * 1024 * 1024


def _params(n_axes):
    return pltpu.CompilerParams(
        dimension_semantics=("arbitrary",) * n_axes,
        vmem_limit_bytes=VMEM_LIMIT_BYTES,
    )


def _rmsnorm_rows(x, g):
    ms = jnp.mean(x * x, axis=-1, keepdims=True)
    return x * lax.rsqrt(ms + EPS) * g


def _ffn_kernel(x_ref, g_ref, wg_ref, wu_ref, wd_ref, fg_ref, o_ref, h_ref, *, final_norm):
    f = pl.program_id(1)

    @pl.when(f == 0)
    def _():
        x = x_ref[...]
        h_ref[...] = _rmsnorm_rows(x, g_ref[...]).astype(BF16)
        o_ref[...] = x

    h = h_ref[...]
    gate = jnp.dot(h, wg_ref[...], preferred_element_type=F32)
    up = jnp.dot(h, wu_ref[...], preferred_element_type=F32)
    act = (gate * (1.0 / (1.0 + jnp.exp(-gate))) * up).astype(BF16)
    o_ref[...] += 0.5 * jnp.dot(act, wd_ref[...], preferred_element_type=F32)

    if final_norm:
        @pl.when(f == pl.num_programs(1) - 1)
        def _():
            o_ref[...] = _rmsnorm_rows(o_ref[...], fg_ref[...])


def _ffn(x, g, wg, wu, wd, final_g, layer, *, final_norm, tm=1024, tf=512):
    t, d = x.shape
    d_ff = wg.shape[2]
    assert t % tm == 0 and d_ff % tf == 0
    return pl.pallas_call(
        functools.partial(_ffn_kernel, final_norm=final_norm),
        grid=(t // tm, d_ff // tf),
        in_specs=[
            pl.BlockSpec((tm, d), lambda i, f: (i, 0)),
            pl.BlockSpec((1, d), lambda i, f: (0, 0)),
            pl.BlockSpec((None, d, tf), lambda i, f: (layer, 0, f)),
            pl.BlockSpec((None, d, tf), lambda i, f: (layer, 0, f)),
            pl.BlockSpec((None, tf, d), lambda i, f: (layer, f, 0)),
            pl.BlockSpec((1, d), lambda i, f: (0, 0)),
        ],
        out_specs=pl.BlockSpec((tm, d), lambda i, f: (i, 0)),
        out_shape=jax.ShapeDtypeStruct((t, d), F32),
        scratch_shapes=[pltpu.VMEM((tm, d), BF16)],
        compiler_params=_params(2),
        name="ffn_final" if final_norm else "ffn",
    )(x, g, wg, wu, wd, final_g)


PROJ_TILE = A_WIDTH
PROJ_A_TILES = 3
PROJ_C_TILES = 3


def _mix_projection_kernel(x_ref, g_ref, w_ref, a_ref, c_ref, u_ref, h_ref):
    j = pl.program_id(1)

    @pl.when(j == 0)
    def _():
        h_ref[...] = _rmsnorm_rows(x_ref[...], g_ref[...]).astype(BF16)

    def product():
        return jnp.dot(h_ref[...], w_ref[...], preferred_element_type=F32)

    @pl.when(j < PROJ_A_TILES)
    def _():
        a_ref[...] = product()

    @pl.when((j >= PROJ_A_TILES) & (j < PROJ_A_TILES + PROJ_C_TILES))
    def _():
        c_ref[...] = product().astype(BF16)

    @pl.when(j == PROJ_A_TILES + PROJ_C_TILES)
    def _():
        u_ref[...] = product()


def _mix_projection(x, g, w, *, tm=1024):
    t, d = x.shape
    tm = min(tm, t)
    n_tiles = PROJ_A_TILES + PROJ_C_TILES + 1
    assert t % tm == 0 and w.shape[1] == n_tiles * PROJ_TILE
    last_a, first_c = PROJ_A_TILES - 1, PROJ_A_TILES
    return pl.pallas_call(
        _mix_projection_kernel,
        grid=(t // tm, n_tiles),
        in_specs=[
            pl.BlockSpec((tm, d), lambda i, j: (i, 0)),
            pl.BlockSpec((1, d), lambda i, j: (0, 0)),
            pl.BlockSpec((d, PROJ_TILE), lambda i, j: (0, j)),
        ],
        out_specs=[
            pl.BlockSpec((tm, PROJ_TILE), lambda i, j: (i, jnp.minimum(j, last_a))),
            pl.BlockSpec((tm, PROJ_TILE),
                         lambda i, j: (i, jnp.clip(j - first_c, 0, PROJ_C_TILES - 1))),
            pl.BlockSpec((tm, PROJ_TILE), lambda i, j: (i, 0)),
        ],
        out_shape=[
            jax.ShapeDtypeStruct((t, PROJ_A_TILES * PROJ_TILE), F32),
            jax.ShapeDtypeStruct((t, PROJ_C_TILES * PROJ_TILE), BF16),
            jax.ShapeDtypeStruct((t, PROJ_TILE), F32),
        ],
        scratch_shapes=[pltpu.VMEM((tm, d), BF16)],
        compiler_params=_params(2),
        name="mix_proj",
    )(x, g, w)


def _head_pair_attention(q2, k2, v2, valid, low_half):
    outs, lses = [], []
    for half in range(2):
        qm = jnp.where(low_half, q2, 0) if half == 0 else jnp.where(low_half, 0, q2)
        s = lax.dot_general(qm, k2, (((1,), (1,)), ((), ())), preferred_element_type=F32)
        s = jnp.where(valid, s, NEG_INF)
        m = jnp.max(s, axis=-1, keepdims=True)
        p = jnp.exp(s - m)
        l = jnp.sum(p, axis=-1, keepdims=True)
        o = jnp.dot(p.astype(BF16), v2, preferred_element_type=F32)
        outs.append(o / l)
        lses.append(jnp.broadcast_to(m + jnp.log(l), (A_BLOCK, LANES)))
    return jnp.where(low_half, outs[0], outs[1]), jnp.where(low_half, lses[0], lses[1])


def _window_mixture_kernel(q_ref, k_ref, v_ref, a_ref, kk_ref, vv_ref, o_ref, lse_ref):
    first = pl.program_id(2) == 0

    @pl.when(first)
    def _():
        kk_ref[0:A_SUPER, :] = jnp.zeros((A_SUPER, LANES), F32)
        vv_ref[0:A_SUPER, :] = jnp.zeros((A_SUPER, LANES), F32)

    kk_ref[A_SUPER:, :] = k_ref[0]
    vv_ref[A_SUPER:, :] = v_ref[0]

    qi = lax.broadcasted_iota(jnp.int32, (A_BLOCK, 2 * A_BLOCK), 0)
    kj = lax.broadcasted_iota(jnp.int32, (A_BLOCK, 2 * A_BLOCK), 1)
    band = kj <= qi + A_BLOCK
    low_half = lax.broadcasted_iota(jnp.int32, (A_BLOCK, LANES), 1) < HEAD_DIM
    n_tasks = A_SUPER // A_BLOCK

    for p, (_, dilation) in enumerate(A_PATTERNS):
        def tasks(g, carry, p=p, dilation=dilation):
            for u in range(TASKS_PER_TRIP):
                t = g * TASKS_PER_TRIP + u
                sub, r = t // dilation, t % dilation
                start = sub * (A_BLOCK * dilation) + r
                lowest = jnp.where(first & (sub == 0), A_BLOCK, qi)
                valid = band & (kj >= lowest)
                q_rows = pl.ds(start, A_BLOCK, stride=dilation)
                kv_rows = pl.ds(A_SUPER + start - A_BLOCK * dilation, 2 * A_BLOCK, stride=dilation)
                q2 = (q_ref[0, q_rows, :] * QK_SCALE).astype(BF16)
                k2 = kk_ref[kv_rows, :].astype(BF16)
                v2 = vv_ref[kv_rows, :].astype(BF16)
                o_ref[p, q_rows, :], lse_ref[p, q_rows, :] = _head_pair_attention(
                    q2, k2, v2, valid, low_half)
            return carry

        lax.fori_loop(0, n_tasks // TASKS_PER_TRIP, tasks, 0)

    kk_ref[0:A_SUPER, :] = k_ref[0]
    vv_ref[0:A_SUPER, :] = v_ref[0]

    def mixture(c, carry):
        rows = pl.ds(pl.multiple_of(c * MIX_ROWS, MIX_ROWS), MIX_ROWS)
        lses = [lse_ref[p, rows, :] for p in range(len(A_PATTERNS))]
        mx = functools.reduce(jnp.maximum, lses)
        es = [jnp.exp(l - mx) for l in lses]
        den = functools.reduce(lambda x, y: x + y, es)
        a = functools.reduce(lambda x, y: x + y,
                             [(e / den) * o_ref[p, rows, :] for p, e in enumerate(es)])
        a_ref[0, rows, :] = a.astype(a_ref.dtype)
        return carry

    lax.fori_loop(0, A_SUPER // MIX_ROWS, mixture, 0)


def _window_mixture(a_qkv, bsz, seq):
    assert seq % A_SUPER == 0 and all(w // d == A_BLOCK for w, d in A_PATTERNS)
    view = a_qkv.reshape(bsz, seq, 3 * A_WIDTH)
    per_part = A_WIDTH // LANES
    part = lambda p: pl.BlockSpec((1, A_SUPER, LANES), lambda b, hp, i: (b, i, p * per_part + hp))
    n_pat = len(A_PATTERNS)
    out = pl.pallas_call(
        _window_mixture_kernel,
        grid=(bsz, per_part, seq // A_SUPER),
        in_specs=[part(0), part(1), part(2)],
        out_specs=part(0),
        out_shape=jax.ShapeDtypeStruct((bsz, seq, A_WIDTH), BF16),
        scratch_shapes=[
            pltpu.VMEM((2 * A_SUPER, LANES), F32),
            pltpu.VMEM((2 * A_SUPER, LANES), F32),
            pltpu.VMEM((n_pat, A_SUPER, LANES), F32),
            pltpu.VMEM((n_pat, A_SUPER, LANES), F32),
        ],
        compiler_params=_params(3),
        name="window_mixture",
    )(view, view, view)
    return out.reshape(bsz * seq, A_WIDTH)


def _diff_attn_kernel(q_ref, k_ref, v_ref, lq1_ref, lk1_ref, lq2_ref, lk2_ref, g_ref, o_ref,
                      qt_ref, vt_ref, s_ref, cmax_ref, m_ref, acc_ref, *, tq, tk, lambda_init):
    qb = pl.program_id(2)
    seq = v_ref.shape[1]

    @pl.when(qb == 0)
    def _():
        def transpose_rows(c, carry):
            off = pl.multiple_of(c * tq, tq)
            vt_ref[0:C_V_DIM, pl.ds(off, tq)] = (
                v_ref[0, pl.ds(off, tq), :].astype(F32).T.astype(BF16))
            return carry
        lax.fori_loop(0, seq // tq, transpose_rows, 0)
        vt_ref[C_V_DIM:, :] = jnp.ones((vt_ref.shape[0] - C_V_DIM, seq), BF16)

    qt = (q_ref[0].astype(F32) * (QK_SCALE * LOG2_E)).T.astype(BF16)
    dim = lax.broadcasted_iota(jnp.int32, (2 * HEAD_DIM, tq), 0)
    zeros = jnp.zeros_like(qt)
    qt_ref[0] = jnp.where(dim < HEAD_DIM, qt, zeros)
    qt_ref[1] = jnp.where(dim < HEAD_DIM, zeros, qt)
    m_ref[...] = jnp.full(m_ref.shape, NEG_INF, F32)
    acc_ref[...] = jnp.zeros(acc_ref.shape, F32)

    def score(slot, key_off, diag_chunk):
        k = k_ref[0, pl.ds(pl.multiple_of(key_off, tk), tk), :]
        for mi in range(2):
            st = jnp.dot(k, qt_ref[mi], preferred_element_type=F32)
            if diag_chunk is not None:
                key = lax.broadcasted_iota(jnp.int32, (tk, tq), 0) + diag_chunk * tk
                qry = lax.broadcasted_iota(jnp.int32, (tk, tq), 1)
                st = jnp.where(key <= qry, st, NEG_INF)
            s_ref[slot, mi] = st
            cmax_ref[slot, mi] = jnp.max(st, axis=0, keepdims=True)

    def absorb(slot, key_off):
        vt = vt_ref[:, pl.ds(pl.multiple_of(key_off, tk), tk)]
        for mi in range(2):
            m_old = m_ref[mi]
            m_new = jnp.maximum(m_old, cmax_ref[slot, mi])
            alpha = jnp.exp2(m_old - m_new)
            pt = jnp.exp2(s_ref[slot, mi] - m_new)
            m_ref[mi] = m_new
            pv = jnp.dot(vt, pt.astype(BF16), preferred_element_type=F32)
            acc_ref[mi] = alpha * acc_ref[mi] + pv

    diag = qb * tq
    score(0, diag, 0)

    def two_blocks(t, carry):
        off = 2 * t * tq
        score(1, off, None)
        absorb(0, jnp.where(t == 0, diag, off - tq))
        score(0, off + tq, None)
        absorb(1, off)
        return carry

    pairs = qb // 2
    lax.fori_loop(0, pairs, two_blocks, 0)
    pending = jnp.where(pairs == 0, diag, 2 * pairs * tq - tq)

    @pl.when(qb % 2 == 1)
    def _():
        score(1, diag - tq, None)
        absorb(0, pending)
        absorb(1, diag - tq)

    @pl.when(qb % 2 == 0)
    def _():
        absorb(0, pending)

    lam = (jnp.exp(jnp.sum(lq1_ref[...] * lk1_ref[...], axis=-1, keepdims=True))
           - jnp.exp(jnp.sum(lq2_ref[...] * lk2_ref[...], axis=-1, keepdims=True))
           + lambda_init)
    l1 = acc_ref[0, C_V_DIM:C_V_DIM + 1, :]
    l2 = acc_ref[1, C_V_DIM:C_V_DIM + 1, :]
    ot = acc_ref[0, 0:C_V_DIM, :] / l1 - lam * (acc_ref[1, 0:C_V_DIM, :] / l2)
    ms = jnp.mean(ot * ot, axis=0, keepdims=True)
    ot = ot * lax.rsqrt(ms + EPS) * g_ref[...] * (1.0 - lambda_init)
    o_ref[0] = ot.T.astype(o_ref.dtype)


def _diff_attention(proj, bsz, seq, lq1, lk1, lq2, lk2, subln_g, lambda_init, *, tq=1024):
    tq = min(tq, seq)
    tk = tq
    assert seq % tq == 0
    cols = proj.shape[-1]
    view = proj.reshape(bsz, seq, cols)
    q0 = 0
    k0 = q0 + C_HEADS
    v0 = k0 + C_HEADS
    vec = lambda: pl.BlockSpec((1, HEAD_DIM), lambda b, h, i: (0, 0))
    out = pl.pallas_call(
        functools.partial(_diff_attn_kernel, tq=tq, tk=tk, lambda_init=lambda_init),
        grid=(bsz, C_HEADS, seq // tq),
        in_specs=[
            pl.BlockSpec((1, tq, LANES), lambda b, h, i: (b, i, q0 + h)),
            pl.BlockSpec((1, seq, LANES), lambda b, h, i: (b, 0, k0 + h)),
            pl.BlockSpec((1, seq, LANES), lambda b, h, i: (b, 0, v0 + h)),
            vec(), vec(), vec(), vec(),
            pl.BlockSpec((C_V_DIM, 1), lambda b, h, i: (0, 0)),
        ],
        out_specs=pl.BlockSpec((1, tq, LANES), lambda b, h, i: (b, i, h)),
        out_shape=jax.ShapeDtypeStruct((bsz, seq, C_WIDTH), BF16),
        scratch_shapes=[
            pltpu.VMEM((2, C_V_DIM, tq), BF16),
            pltpu.VMEM((C_V_DIM + BF16_ROWS, seq), BF16),
            pltpu.VMEM((2, 2, tk, tq), F32),
            pltpu.VMEM((2, 2, 1, tq), F32),
            pltpu.VMEM((2, 1, tq), F32),
            pltpu.VMEM((2, C_V_DIM + BF16_ROWS, tq), F32),
        ],
        compiler_params=_params(3),
        name="diff_attn",
    )(view, view, view, lq1, lk1, lq2, lk2, subln_g.reshape(C_V_DIM, 1))
    return out.reshape(bsz * seq, C_WIDTH)


def _mix_out_kernel(x_ref, a_ref, u_ref, halo_ref, c_ref, pw_ref, ps_ref, wo_ref, out_ref,
                    *, tm, rows_per_seq):
    i = pl.program_id(0)

    acc = jnp.dot(a_ref[...], wo_ref[0:A_WIDTH, :], preferred_element_type=F32)

    t_in_seq = (i * tm) % rows_per_seq + lax.broadcasted_iota(jnp.int32, (tm, 1), 0)
    first = (i * tm) % rows_per_seq == 0
    halo = jnp.where(first, jnp.zeros((B_HALO, B_WIDTH), F32), halo_ref[...])
    u = u_ref[...]
    ext = jnp.concatenate([halo, u], axis=0)
    ys = []
    win = ext
    width = 1
    for g, w in enumerate(B_WINDOWS):
        while width < w:
            win = win + pltpu.roll(win, width, 0)
            width *= 2
        gs = slice(g * B_GROUP_DIM, (g + 1) * B_GROUP_DIM)
        cnt = jnp.minimum(t_in_seq + 1, w).astype(F32)
        diff = win[B_HALO:, gs] / cnt - u[:, gs]
        ys.append(jnp.dot(diff.astype(BF16), pw_ref[g], preferred_element_type=F32))
    b = jnp.concatenate(ys, axis=-1) * ps_ref[...]
    acc += jnp.dot(b.astype(BF16), wo_ref[A_WIDTH:A_WIDTH + B_WIDTH, :], preferred_element_type=F32)

    acc += jnp.dot(c_ref[...], wo_ref[A_WIDTH + B_WIDTH:, :], preferred_element_type=F32)
    out_ref[...] = x_ref[...] + acc


def _mix_out(x, a, u, c, pool_w, pool_scale, w_out, seq, *, tm=256):
    t, d = x.shape
    assert t % tm == 0 and seq % tm == 0 and tm % B_HALO == 0
    row = lambda width: pl.BlockSpec((tm, width), lambda i: (i, 0))
    halo_blocks = tm // B_HALO
    return pl.pallas_call(
        functools.partial(_mix_out_kernel, tm=tm, rows_per_seq=seq),
        grid=(t // tm,),
        in_specs=[
            row(d),
            row(A_WIDTH),
            row(B_WIDTH),
            pl.BlockSpec((B_HALO, B_WIDTH), lambda i: (jnp.maximum(i * halo_blocks - 1, 0), 0)),
            row(C_WIDTH),
            pl.BlockSpec(pool_w.shape, lambda i: (0, 0, 0)),
            pl.BlockSpec((1, B_WIDTH), lambda i: (0, 0)),
            pl.BlockSpec(w_out.shape, lambda i: (0, 0)),
        ],
        out_specs=row(d),
        out_shape=jax.ShapeDtypeStruct((t, d), F32),
        compiler_params=_params(1),
        name="mix_out",
    )(x, a, u, u, c, pool_w, pool_scale, w_out)


def kernel(x, ffn1_norm, ffn1_w_gate, ffn1_w_up, ffn1_w_down, mix_norm, w_in, pool_w, pool_scale,
           diff_lambda_q1, diff_lambda_k1, diff_lambda_q2, diff_lambda_k2, diff_subln, w_out,
           ffn2_norm, ffn2_w_gate, ffn2_w_up, ffn2_w_down, final_norm):
    bsz, seq, d = x.shape
    depth = w_in.shape[0]
    b_lo, c_lo = 3 * A_WIDTH, 3 * A_WIDTH + B_WIDTH
    assert w_in.shape[2] == c_lo + 3 * C_WIDTH and d == A_WIDTH + B_WIDTH + C_WIDTH
    xt = x.reshape(bsz * seq, d)
    row = lambda v: v.reshape(1, -1).astype(F32)
    ffn1 = tuple(w.astype(BF16) for w in (ffn1_w_gate, ffn1_w_up, ffn1_w_down))
    ffn2 = tuple(w.astype(BF16) for w in (ffn2_w_gate, ffn2_w_up, ffn2_w_down))
    for l in range(depth):
        xt = _ffn(xt, row(ffn1_norm[l]), *ffn1, row(final_norm), l, final_norm=False)

        w_l = w_in[l].astype(BF16)
        w_mix = jnp.concatenate([w_l[:, :b_lo], w_l[:, c_lo:], w_l[:, b_lo:c_lo],
                                 jnp.zeros((d, PROJ_TILE - B_WIDTH), BF16)], axis=1)
        a_qkv, c_qkv, u = _mix_projection(xt, row(mix_norm[l]), w_mix)

        a = _window_mixture(a_qkv, bsz, seq)

        lambda_init = 0.8 - 0.6 * math.exp(-0.3 * l)
        c = _diff_attention(c_qkv, bsz, seq, row(diff_lambda_q1[l]), row(diff_lambda_k1[l]),
                            row(diff_lambda_q2[l]), row(diff_lambda_k2[l]), row(diff_subln[l]),
                            lambda_init)

        xt = _mix_out(xt, a, u, c, pool_w[l].astype(BF16), row(pool_scale[l]),
                      w_out[l].astype(BF16), seq)

        xt = _ffn(xt, row(ffn2_norm[l]), *ffn2, row(final_norm), l,
                  final_norm=(l == depth - 1))
    return xt.reshape(bsz, seq, d)
```
